```python
import math
import jax, jax.numpy as jnp
from jax import lax
import numpy as np

D_MODEL = 1024
BATCH = 8
SEQ = 4096
DEPTH = 4

CHUNK = 64
Q_BLOCK = 128
EPS = 1e-6
A_HEADS = 4
A_QK_DIM = 64
A_V_DIM = 2 * A_QK_DIM
A_WIDTH = A_HEADS * A_V_DIM
B_HEADS = 4
B_K_DIM = 64
B_V_DIM = 128
B_WIDTH = B_HEADS * B_V_DIM
B_GATE_RANK = 16
B_GATE_TAU = 16.0
BRANCH_WIDTH = A_WIDTH
N_BRANCH = 2
D_FF = 4 * D_MODEL
SPLITS = (A_HEADS * 2 * A_QK_DIM,
          A_HEADS * 2 * A_QK_DIM,
          A_WIDTH,
          B_HEADS * B_K_DIM,
          B_HEADS * B_K_DIM,
          B_WIDTH,
          B_GATE_RANK,
          B_WIDTH,
          N_BRANCH * D_MODEL)
D_IN = sum(SPLITS)

kernel_name = "hybrid_diffattn_gla_gated_merge"


def rms_norm(x, gain):
    xf = x.astype(jnp.float32)
    y = xf * lax.rsqrt(jnp.mean(xf * xf, axis=-1, keepdims=True) + EPS)
    return (y * gain.astype(jnp.float32)).astype(x.dtype)


def split_columns(proj):
    outs, start = [], 0
    for width in SPLITS:
        outs.append(proj[..., start:start + width])
        start += width
    return outs


def diff_attention(aq, ak, av, q_gain, k_gain, lq1, lk1, lq2, lk2, sub_gain, lambda_init):
    B, S = aq.shape[:2]
    q = rms_norm(aq.reshape(B, S, A_HEADS, 2, A_QK_DIM), q_gain)
    k = rms_norm(ak.reshape(B, S, A_HEADS, 2, A_QK_DIM), k_gain)
    v = av.reshape(B, S, A_HEADS, A_V_DIM)
    lam = (jnp.exp(jnp.sum(lq1.astype(jnp.float32) * lk1.astype(jnp.float32)))
           - jnp.exp(jnp.sum(lq2.astype(jnp.float32) * lk2.astype(jnp.float32)))
           + lambda_init)
    scale = A_QK_DIM ** -0.5
    n_blk = S // Q_BLOCK
    qb = q.reshape(B, n_blk, Q_BLOCK, A_HEADS, 2, A_QK_DIM).transpose(1, 0, 2, 3, 4, 5)
    key_chunk = jnp.arange(S) // CHUNK

    def block(args):
        qi, idx = args
        scores = jnp.einsum('bqhmd,bkhmd->bhmqk', qi, k).astype(jnp.float32) * scale
        q_chunk = (idx * Q_BLOCK + jnp.arange(Q_BLOCK)) // CHUNK
        mask = key_chunk[None, :] <= q_chunk[:, None]
        scores = jnp.where(mask, scores, -jnp.inf)
        p = jax.nn.softmax(scores, axis=-1)
        p = p[:, :, 0] - lam * p[:, :, 1]
        return jnp.einsum('bhqk,bkhe->bqhe', p.astype(v.dtype), v)

    out = lax.map(block, (qb, jnp.arange(n_blk)))
    out = out.transpose(1, 0, 2, 3, 4).reshape(B, S, A_HEADS, A_V_DIM)
    out = rms_norm(out, sub_gain) * (1.0 - lambda_init)
    return out.reshape(B, S, A_WIDTH)


def gla(bq, bk, bv, ba, br, w_gate_up, b_gate, out_gain):
    B, S = bq.shape[:2]
    nC = S // CHUNK
    f32 = jnp.float32
    z = (ba @ w_gate_up + b_gate).astype(f32)
    g = (jax.nn.log_sigmoid(z) / B_GATE_TAU).reshape(B, nC, CHUNK, B_HEADS, B_K_DIM)
    G = jnp.cumsum(g, axis=2)
    q = bq.reshape(B, nC, CHUNK, B_HEADS, B_K_DIM).astype(f32) * (B_K_DIM ** -0.5)
    k = bk.reshape(B, nC, CHUNK, B_HEADS, B_K_DIM).astype(f32)
    v = bv.reshape(B, nC, CHUNK, B_HEADS, B_V_DIM).astype(f32)
    q_dec = q * jnp.exp(G)
    k_dec = k * jnp.exp(-G)
    causal = jnp.tril(jnp.ones((CHUNK, CHUNK), dtype=bool))
    a = jnp.einsum('bnihd,bnjhd->bnhij', q_dec, k_dec)
    a = jnp.where(causal, a, 0.0)
    o_intra = jnp.einsum('bnhij,bnjhe->bnihe', a, v)
    G_last = G[:, :, -1]
    k_state = k * jnp.exp(G_last[:, :, None] - G)
    chunk_kv = jnp.einsum('bnjhd,bnjhe->bnhde', k_state, v)

    def step(state, inp):
        decay, kv = inp
        return decay[..., None] * state + kv, state

    init = jnp.zeros((B, B_HEADS, B_K_DIM, B_V_DIM), f32)
    _, states = lax.scan(step, init, (jnp.exp(G_last).transpose(1, 0, 2, 3),
                                      chunk_kv.transpose(1, 0, 2, 3, 4)))
    states = states.transpose(1, 0, 2, 3, 4)
    o_inter = jnp.einsum('bnihd,bnhde->bnihe', q_dec, states)
    o = (o_intra + o_inter).reshape(B, S, B_HEADS, B_V_DIM)
    r = br.reshape(B, S, B_HEADS, B_V_DIM).astype(f32)
    o = rms_norm(o, out_gain) * jax.nn.silu(r)
    return o.reshape(B, S, B_WIDTH).astype(bq.dtype)


def setup_inputs(seed: int = 0) -> dict:
    key = jax.random.key(seed)
    ks = jax.random.split(key, 18)
    L, D = DEPTH, D_MODEL

    def nrm(k, shape, scale):
        return jax.random.normal(k, shape, jnp.float32) * scale

    def gain(k, shape):
        return 1.0 + 0.02 * jax.random.normal(k, shape, jnp.float32)

    return {
        "x": nrm(ks[0], (BATCH, SEQ, D), 1.0),
        "norm_mix": gain(ks[1], (L, D)),
        "w_in": nrm(ks[2], (L, D, D_IN), D ** -0.5),
        "a_q_norm": gain(ks[3], (L, A_QK_DIM)),
        "a_k_norm": gain(ks[4], (L, A_QK_DIM)),
        "a_lambda_q1": nrm(ks[5], (L, A_QK_DIM), 0.1),
        "a_lambda_k1": nrm(ks[6], (L, A_QK_DIM), 0.1),
        "a_lambda_q2": nrm(ks[7], (L, A_QK_DIM), 0.1),
        "a_lambda_k2": nrm(ks[8], (L, A_QK_DIM), 0.1),
        "a_sub_norm": gain(ks[9], (L, A_V_DIM)),
        "b_gate_up": nrm(ks[10], (L, B_GATE_RANK, B_HEADS * B_K_DIM), B_GATE_RANK ** -0.5),
        "b_gate_bias": nrm(ks[11], (L, B_HEADS * B_K_DIM), 0.1),
        "b_out_norm": gain(ks[12], (L, B_V_DIM)),
        "w_branch": nrm(ks[13], (L, N_BRANCH, BRANCH_WIDTH, D), BRANCH_WIDTH ** -0.5),
        "w_out": nrm(ks[14], (L, D, D), D ** -0.5),
        "norm_ffn": gain(ks[15], (L, D)),
        "w_up": nrm(ks[16], (L, D, D_FF), D ** -0.5),
        "w_down": nrm(ks[17], (L, D_FF, D), 0.5 * D_FF ** -0.5),
    }


def reference(x, norm_mix, w_in, a_q_norm, a_k_norm, a_lambda_q1, a_lambda_k1,
              a_lambda_q2, a_lambda_k2, a_sub_norm, b_gate_up, b_gate_bias, b_out_norm,
              w_branch, w_out, norm_ffn, w_up, w_down):
    B, S, D = x.shape
    for l in range(DEPTH):
        lambda_init = 0.8 - 0.6 * math.exp(-0.3 * l)
        u = rms_norm(x, norm_mix[l])
        proj = u @ w_in[l]
        aq, ak, av, bq, bk, bv, ba, br, gates = split_columns(proj)
        ya = diff_attention(aq, ak, av, a_q_norm[l], a_k_norm[l], a_lambda_q1[l],
                            a_lambda_k1[l], a_lambda_q2[l], a_lambda_k2[l],
                            a_sub_norm[l], lambda_init)
        yb = gla(bq, bk, bv, ba, br, b_gate_up[l], b_gate_bias[l], b_out_norm[l])
        branches = jnp.stack([ya, yb], axis=2)
        branch_proj = jnp.einsum('bsnw,nwd->bsnd', branches, w_branch[l])
        gate = jax.nn.sigmoid(gates.reshape(B, S, N_BRANCH, D))
        mixed = jnp.sum(gate * branch_proj, axis=2) @ w_out[l]
        x = x + mixed
        h = rms_norm(x, norm_ffn[l])
        x = x + jnp.square(jax.nn.relu(h @ w_up[l])) @ w_down[l]
    return x
```

```python
import functools
import math

import jax
import jax.numpy as jnp
from jax import lax
from jax.experimental import pallas as pl
from jax.experimental.pallas import tpu as pltpu

D_MODEL = 1024
DEPTH = 4
CHUNK = 64
EPS = 1e-6
A_HEADS = 4
A_QK_DIM = 64
A_V_DIM = 128
A_WIDTH = 512
B_HEADS = 4
B_K_DIM = 64
B_V_DIM = 128
B_WIDTH = 512
B_GATE_RANK = 16
B_GATE_TAU = 16.0
D_FF = 4 * D_MODEL

VMEM_LIMIT_BYTES = 56 * 1024 * 1024
LANES = 128

TM_PROJ = 512
TQ = 256
TK = 256
PAIR = 2 * CHUNK
TG = 512
TM_MERGE = 512
FF_CHUNK = 1024
NEG_BIG = -1e30

BF16 = jnp.bfloat16
F32 = jnp.float32


def _dot(a, b):
    return jnp.dot(a, b, preferred_element_type=F32)


def _dot_nt(a, b):
    return lax.dot_general(a, b, (((1,), (1,)), ((), ())), preferred_element_type=F32)


def _in_proj_body(x_ref, nw_ref, wtok_ref, wfeat_ref, wgu_ref, wgut_ref, bg_ref, bgt_ref,
                  ak_ref, bq_ref, bv_ref, br_ref, gates_ref, z_ref,
                  aqt_ref, avt_ref, bkt_ref, zt_ref):
    x = x_ref[0]
    ms = jnp.mean(x * x, axis=-1, keepdims=True)
    u = (x * lax.rsqrt(ms + EPS) * nw_ref[...]).astype(BF16)

    def tok(start, width):
        return _dot(u, wtok_ref[:, start:start + width])

    ak_ref[0] = tok(0, 512).astype(BF16)
    bq_ref[0] = tok(512, 256).astype(BF16)
    bv_ref[0] = tok(768, 512).astype(BF16)
    br_ref[0] = tok(1280, 512).astype(BF16)
    for c in range(4):
        gates_ref[0, :, c * 512:(c + 1) * 512] = tok(1792 + c * 512, 512).astype(BF16)
    ba = tok(3840, LANES).astype(BF16)
    z_ref[0] = _dot(ba, wgu_ref[...]) + bg_ref[...]

    def feat(start, rows):
        return _dot_nt(wfeat_ref[start:start + rows, :], u)

    n_q = TM_PROJ // TQ
    aqt = feat(0, 512).astype(BF16)
    avt = feat(512, 512).astype(BF16)
    for s in range(n_q):
        aqt_ref[0, s] = aqt[:, s * TQ:(s + 1) * TQ]
        avt_ref[0, s] = avt[:, s * TQ:(s + 1) * TQ]
    bkt = feat(1024, 256).astype(BF16)
    bat = feat(1280, B_GATE_RANK).astype(BF16)
    zt = _dot(wgut_ref[...], bat) + bgt_ref[...]
    for s in range(TM_PROJ // PAIR):
        bkt_ref[0, s] = bkt[:, s * PAIR:(s + 1) * PAIR]
        zt_ref[0, s] = zt[:, s * PAIR:(s + 1) * PAIR]


def _in_proj(x, nw, wtok, wfeat, wgu, wgut, bg, bgt):
    B, S, D = x.shape
    tm = TM_PROJ
    grid = (B, S // tm)
    const2 = lambda b, i: (0, 0)
    tokspec = lambda w: pl.BlockSpec((1, tm, w), lambda b, i: (b, i, 0))
    out_shape = (
        jax.ShapeDtypeStruct((B, S, 512), BF16),
        jax.ShapeDtypeStruct((B, S, 256), BF16),
        jax.ShapeDtypeStruct((B, S, 512), BF16),
        jax.ShapeDtypeStruct((B, S, 512), BF16),
        jax.ShapeDtypeStruct((B, S, 2 * D_MODEL), BF16),
        jax.ShapeDtypeStruct((B, S, 256), F32),
        jax.ShapeDtypeStruct((B, S // TQ, 512, TQ), BF16),
        jax.ShapeDtypeStruct((B, S // TQ, 512, TQ), BF16),
        jax.ShapeDtypeStruct((B, S // PAIR, 256, PAIR), BF16),
        jax.ShapeDtypeStruct((B, S // PAIR, 256, PAIR), F32),
    )
    out_specs = (
        tokspec(512), tokspec(256), tokspec(512), tokspec(512), tokspec(2 * D_MODEL), tokspec(256),
        pl.BlockSpec((1, tm // TQ, 512, TQ), lambda b, i: (b, i, 0, 0)),
        pl.BlockSpec((1, tm // TQ, 512, TQ), lambda b, i: (b, i, 0, 0)),
        pl.BlockSpec((1, tm // PAIR, 256, PAIR), lambda b, i: (b, i, 0, 0)),
        pl.BlockSpec((1, tm // PAIR, 256, PAIR), lambda b, i: (b, i, 0, 0)),
    )
    in_specs = [
        pl.BlockSpec((1, tm, D), lambda b, i: (b, i, 0)),
        pl.BlockSpec(nw.shape, const2),
        pl.BlockSpec(wtok.shape, const2),
        pl.BlockSpec(wfeat.shape, const2),
        pl.BlockSpec(wgu.shape, const2),
        pl.BlockSpec(wgut.shape, const2),
        pl.BlockSpec(bg.shape, const2),
        pl.BlockSpec(bgt.shape, const2),
    ]
    return pl.pallas_call(
        _in_proj_body,
        out_shape=out_shape,
        grid=grid,
        in_specs=in_specs,
        out_specs=out_specs,
        compiler_params=pltpu.CompilerParams(
            dimension_semantics=("parallel", "arbitrary"),
            vmem_limit_bytes=VMEM_LIMIT_BYTES),
        name="in_proj",
    )(x, nw, wtok, wfeat, wgu, wgut, bg, bgt)


def _attn_body(linit_ref, lam_ref, gq_ref, gk_ref, sg_ref, qt_ref, k_ref, vt_ref, o_ref,
               khat_scr, qbd_scr, m_scr, l_scr, acc_scr):
    i = pl.program_id(2)
    n_kv = khat_scr.shape[0]
    lane = lax.broadcasted_iota(jnp.int32, (TK, LANES), 1)
    first_map = lane < A_QK_DIM

    @pl.when(i == 0)
    def _():
        def norm_blk(j, carry):
            kk = k_ref[0, pl.ds(pl.multiple_of(j * TK, TK), TK), :].astype(F32)
            k2 = kk * kk
            s1 = jnp.sum(jnp.where(first_map, k2, 0.0), axis=-1, keepdims=True)
            s2 = jnp.sum(jnp.where(first_map, 0.0, k2), axis=-1, keepdims=True)
            r = jnp.where(first_map, lax.rsqrt(s1 * (1.0 / A_QK_DIM) + EPS),
                          lax.rsqrt(s2 * (1.0 / A_QK_DIM) + EPS))
            khat_scr[j] = (kk * r * gk_ref[...]).astype(BF16)
            return carry
        lax.fori_loop(0, n_kv, norm_blk, 0)

    q = qt_ref[0, 0].astype(F32)
    q2 = q * q
    scale = A_QK_DIM ** -0.5
    r1 = lax.rsqrt(jnp.mean(q2[:A_QK_DIM], axis=0, keepdims=True) + EPS)
    r2 = lax.rsqrt(jnp.mean(q2[A_QK_DIM:], axis=0, keepdims=True) + EPS)
    gq = gq_ref[...] * scale
    zeros = jnp.zeros((A_QK_DIM, TQ), BF16)
    qbd_scr[:A_QK_DIM, :TQ] = (q[:A_QK_DIM] * r1 * gq[:A_QK_DIM]).astype(BF16)
    qbd_scr[:A_QK_DIM, TQ:] = zeros
    qbd_scr[A_QK_DIM:, :TQ] = zeros
    qbd_scr[A_QK_DIM:, TQ:] = (q[A_QK_DIM:] * r2 * gq[A_QK_DIM:]).astype(BF16)

    m_scr[...] = jnp.full(m_scr.shape, NEG_BIG, F32)
    l_scr[...] = jnp.zeros(l_scr.shape, F32)
    acc_scr[...] = jnp.zeros(acc_scr.shape, F32)

    def block(j, masked):
        s = _dot(khat_scr[j], qbd_scr[...])
        if masked:
            kc = lax.shift_right_logical(lax.broadcasted_iota(jnp.int32, s.shape, 0), 6)
            qc = lax.shift_right_logical(
                lax.broadcasted_iota(jnp.int32, s.shape, 1) & (TQ - 1), 6)
            s = jnp.where(kc <= qc, s, NEG_BIG)
        m_old = m_scr[...]
        m_new = jnp.maximum(m_old, jnp.max(s, axis=0, keepdims=True))
        alpha = jnp.exp(m_old - m_new)
        p = jnp.exp(s - m_new)
        l_scr[...] = alpha * l_scr[...] + jnp.sum(p, axis=0, keepdims=True)
        m_scr[...] = m_new
        pv = _dot(vt_ref[0, j], p.astype(BF16))
        acc_scr[...] = alpha * acc_scr[...] + pv

    def full_block(j, carry):
        block(j, False)
        return carry

    lax.fori_loop(0, i, full_block, 0)
    block(i, True)

    lp = lam_ref[...]
    d1 = jnp.sum(lp[0:1] * lp[1:2], axis=-1, keepdims=True)
    d2 = jnp.sum(lp[2:3] * lp[3:4], axis=-1, keepdims=True)
    linit = linit_ref[0]
    lam = jnp.exp(d1) - jnp.exp(d2) + linit
    inv_l = 1.0 / l_scr[...]
    acc = acc_scr[...]
    o = acc[:, :TQ] * inv_l[:, :TQ] - lam * (acc[:, TQ:] * inv_l[:, TQ:])
    ms = jnp.mean(o * o, axis=0, keepdims=True)
    y = o * lax.rsqrt(ms + EPS) * sg_ref[...] * (1.0 - linit)
    o_ref[0] = y.T.astype(BF16)


def _attn(linit, lam_params, gq_col, gk_row, sg_col, aqt, ak, avt):
    B, S, _ = ak.shape
    n_q = S // TQ
    n_kv = S // TK
    grid = (B, A_HEADS, n_q)
    const2 = lambda b, h, i: (0, 0)
    in_specs = [
        pl.BlockSpec(memory_space=pltpu.SMEM),
        pl.BlockSpec(lam_params.shape, const2),
        pl.BlockSpec(gq_col.shape, const2),
        pl.BlockSpec(gk_row.shape, const2),
        pl.BlockSpec(sg_col.shape, const2),
        pl.BlockSpec((1, 1, LANES, TQ), lambda b, h, i: (b, i, h, 0)),
        pl.BlockSpec((1, S, LANES), lambda b, h, i: (b, 0, h)),
        pl.BlockSpec((1, n_kv, LANES, TK), lambda b, h, i: (b, 0, h, 0)),
    ]
    return pl.pallas_call(
        _attn_body,
        out_shape=jax.ShapeDtypeStruct((B, S, A_WIDTH), BF16),
        grid=grid,
        in_specs=in_specs,
        out_specs=pl.BlockSpec((1, TQ, LANES), lambda b, h, i: (b, i, h)),
        scratch_shapes=[
            pltpu.VMEM((n_kv, TK, LANES), BF16),
            pltpu.VMEM((LANES, 2 * TQ), BF16),
            pltpu.VMEM((1, 2 * TQ), F32),
            pltpu.VMEM((1, 2 * TQ), F32),
            pltpu.VMEM((LANES, 2 * TQ), F32),
        ],
        compiler_params=pltpu.CompilerParams(
            dimension_semantics=("parallel", "parallel", "arbitrary"),
            vmem_limit_bytes=VMEM_LIMIT_BYTES),
        name="diff_attn",
    )(linit, lam_params, gq_col, gk_row, sg_col, aqt, ak, avt)


def _log_sigmoid(z):
    return jnp.minimum(z, 0.0) - jnp.log(1.0 + jnp.exp(-jnp.abs(z)))


def _split3(g):
    hi = g.astype(BF16)
    r1 = g - hi.astype(F32)
    mid = r1.astype(BF16)
    lo = (r1 - mid.astype(F32)).astype(BF16)
    return hi, mid, lo


def _gla_body(bq_ref, bkt_ref, bv_ref, z_ref, zt_ref, br_ref, og_ref, o_ref, state_scr):
    @pl.when(pl.program_id(1) == 0)
    def _():
        state_scr[...] = jnp.zeros(state_scr.shape, F32)

    r = lax.broadcasted_iota(jnp.int32, (PAIR, PAIR), 0)
    c = lax.broadcasted_iota(jnp.int32, (PAIR, PAIR), 1)
    same = lax.shift_right_logical(r, 6) == lax.shift_right_logical(c, 6)
    tril_mask = same & (c <= r)
    tril = jnp.where(tril_mask, 1.0, 0.0).astype(BF16)
    triu = jnp.where(same & (r <= c), 1.0, 0.0).astype(BF16)
    lane_q = lax.broadcasted_iota(jnp.int32, (PAIR, B_HEADS * B_K_DIM), 1)
    lane_t = lax.broadcasted_iota(jnp.int32, (B_HEADS * B_K_DIM, PAIR), 1)
    first_chunk = lane_t < CHUNK
    inv_tau = 1.0 / B_GATE_TAU

    def pair_step(p, carry):
        rows = pl.ds(pl.multiple_of(p * PAIR, PAIR), PAIR)
        g = _log_sigmoid(z_ref[0, rows, :]) * inv_tau
        gt = _log_sigmoid(zt_ref[0, p]) * inv_tau
        g_hi, g_mid, g_lo = _split3(g)
        G = _dot(tril, g_hi) + _dot(tril, g_mid) + _dot(tril, g_lo)
        t_hi, t_mid, t_lo = _split3(gt)
        Gt = _dot(t_hi, triu) + _dot(t_mid, triu) + _dot(t_lo, triu)
        gl0 = jnp.sum(jnp.where(first_chunk, gt, 0.0), axis=-1, keepdims=True)
        gl1 = jnp.sum(jnp.where(first_chunk, 0.0, gt), axis=-1, keepdims=True)
        glast = jnp.where(first_chunk, gl0, gl1)

        q_dec = (bq_ref[0, rows, :].astype(F32) * (B_K_DIM ** -0.5) * jnp.exp(G)).astype(BF16)
        kt = bkt_ref[0, p].astype(F32)
        kdt = (kt * jnp.exp(-Gt)).astype(BF16)
        kst = kt * jnp.exp(glast - Gt)
        kst0 = jnp.where(first_chunk, kst, 0.0).astype(BF16)
        kst1 = jnp.where(first_chunk, 0.0, kst).astype(BF16)
        v = bv_ref[0, rows, :]

        s0 = state_scr[...]
        kv0 = jnp.concatenate(
            [_dot(kst0[h * B_K_DIM:(h + 1) * B_K_DIM], v[:, h * B_V_DIM:(h + 1) * B_V_DIM])
             for h in range(B_HEADS)], axis=0)
        kv1 = jnp.concatenate(
            [_dot(kst1[h * B_K_DIM:(h + 1) * B_K_DIM], v[:, h * B_V_DIM:(h + 1) * B_V_DIM])
             for h in range(B_HEADS)], axis=0)
        s1 = jnp.exp(gl0) * s0 + kv0
        s2 = jnp.exp(gl1) * s1 + kv1
        state_scr[...] = s2
        s0b = s0.astype(BF16)
        s1b = s1.astype(BF16)

        og = og_ref[...]
        for h in range(B_HEADS):
            head = (lane_q >= h * B_K_DIM) & (lane_q < (h + 1) * B_K_DIM)
            qh = jnp.where(head, q_dec, jnp.zeros_like(q_dec))
            a = _dot(qh, kdt)
            a = jnp.where(tril_mask, a, 0.0).astype(BF16)
            vh = v[:, h * B_V_DIM:(h + 1) * B_V_DIM]
            o_intra = _dot(a, vh)
            o_inter = jnp.concatenate(
                [_dot(qh[:CHUNK], s0b), _dot(qh[CHUNK:], s1b)], axis=0)
            o = o_intra + o_inter
            ms = jnp.mean(o * o, axis=-1, keepdims=True)
            rr = br_ref[0, rows, h * B_V_DIM:(h + 1) * B_V_DIM].astype(F32)
            silu = rr / (1.0 + jnp.exp(-rr))
            o_ref[0, rows, h * B_V_DIM:(h + 1) * B_V_DIM] = (
                o * lax.rsqrt(ms + EPS) * og * silu).astype(BF16)
        return carry

    lax.fori_loop(0, TG // PAIR, pair_step, 0)


def _gla(bq, bkt, bv, z, zt, br, og):
    B, S, _ = bq.shape
    grid = (B, S // TG)
    tokspec = lambda w: pl.BlockSpec((1, TG, w), lambda b, i: (b, i, 0))
    featspec = pl.BlockSpec((1, TG // PAIR, 256, PAIR), lambda b, i: (b, i, 0, 0))
    return pl.pallas_call(
        _gla_body,
        out_shape=jax.ShapeDtypeStruct((B, S, B_WIDTH), BF16),
        grid=grid,
        in_specs=[tokspec(256), featspec, tokspec(512), tokspec(256), featspec, tokspec(512),
                  pl.BlockSpec(og.shape, lambda b, i: (0, 0))],
        out_specs=tokspec(512),
        scratch_shapes=[pltpu.VMEM((B_HEADS * B_K_DIM, B_V_DIM), F32)],
        compiler_params=pltpu.CompilerParams(
            dimension_semantics=("parallel", "arbitrary"),
            vmem_limit_bytes=VMEM_LIMIT_BYTES),
        name="gla",
    )(bq, bkt, bv, z, zt, br, og)


def _merge_body(x_ref, ya_ref, yb_ref, g_ref, wb_ref, wo_ref, nf_ref, wu_ref, wd_ref, o_ref):
    pa = _dot(ya_ref[...], wb_ref[0])
    pb = _dot(yb_ref[...], wb_ref[1])
    ga = 1.0 / (1.0 + jnp.exp(-g_ref[:, :D_MODEL].astype(F32)))
    gb = 1.0 / (1.0 + jnp.exp(-g_ref[:, D_MODEL:].astype(F32)))
    mixed = (ga * pa + gb * pb).astype(BF16)
    x1 = x_ref[...] + _dot(mixed, wo_ref[...])
    ms = jnp.mean(x1 * x1, axis=-1, keepdims=True)
    h = (x1 * lax.rsqrt(ms + EPS) * nf_ref[...]).astype(BF16)
    acc = x1
    for c in range(D_FF // FF_CHUNK):
        up = _dot(h, wu_ref[:, c * FF_CHUNK:(c + 1) * FF_CHUNK])
        act = jnp.square(jnp.maximum(up, 0.0)).astype(BF16)
        acc = acc + _dot(act, wd_ref[c * FF_CHUNK:(c + 1) * FF_CHUNK, :])
    o_ref[...] = acc


def _merge(x2, ya2, yb2, gates2, wb, wo, nf, wu, wd):
    T, D = x2.shape
    tm = TM_MERGE
    row = lambda w: pl.BlockSpec((tm, w), lambda i: (i, 0))
    const = lambda a: pl.BlockSpec(a.shape, lambda i: (0,) * a.ndim,
                                   pipeline_mode=pl.Buffered(1))
    return pl.pallas_call(
        _merge_body,
        out_shape=jax.ShapeDtypeStruct((T, D), F32),
        grid=(T // tm,),
        in_specs=[row(D), row(A_WIDTH), row(B_WIDTH), row(2 * D),
                  const(wb), const(wo), const(nf), const(wu), const(wd)],
        out_specs=row(D),
        compiler_params=pltpu.CompilerParams(
            dimension_semantics=("parallel",),
            vmem_limit_bytes=VMEM_LIMIT_BYTES),
        name="merge_mlp",
    )(x2, ya2, yb2, gates2, wb, wo, nf, wu, wd)


def _split_w_in(w_in):
    widths = (512, 512, 512, 256, 256, 512, B_GATE_RANK, 512, 2 * D_MODEL)
    parts, start = [], 0
    for w in widths:
        parts.append(w_in[..., start:start + w])
        start += w
    return parts


def kernel(x, norm_mix, w_in, a_q_norm, a_k_norm, a_lambda_q1, a_lambda_k1, a_lambda_q2,
           a_lambda_k2, a_sub_norm, b_gate_up, b_gate_bias, b_out_norm, w_branch, w_out,
           norm_ffn, w_up, w_down):
    B, S, D = x.shape
    L = DEPTH
    aq, ak, av, bq, bk, bv, ba, br, gates = _split_w_in(w_in)
    ba_pad = jnp.pad(ba, ((0, 0), (0, 0), (0, LANES - B_GATE_RANK)))
    wtok = jnp.concatenate([ak, bq, bv, br, gates, ba_pad], axis=-1).astype(BF16)
    wfeat = jnp.swapaxes(jnp.concatenate([aq, av, bk, ba], axis=-1), 1, 2).astype(BF16)
    wgu = jnp.pad(b_gate_up, ((0, 0), (0, LANES - B_GATE_RANK), (0, 0))).astype(BF16)
    wgut = jnp.swapaxes(b_gate_up, 1, 2).astype(BF16)
    bg = b_gate_bias[:, None, :]
    bgt = b_gate_bias[:, :, None]
    nw = norm_mix[:, None, :]
    nf = norm_ffn[:, None, :]
    gq_col = jnp.tile(a_q_norm, (1, 2))[:, :, None]
    gk_row = jnp.tile(a_k_norm, (1, 2))[:, None, :]
    sg_col = a_sub_norm[:, :, None]
    og_row = b_out_norm[:, None, :]
    lam_params = jnp.stack([a_lambda_q1, a_lambda_k1, a_lambda_q2, a_lambda_k2], axis=1)
    wb = w_branch.astype(BF16)
    wo = w_out.astype(BF16)
    wu = w_up.astype(BF16)
    wd = w_down.astype(BF16)

    for l in range(L):
        linit = jnp.full((1,), 0.8 - 0.6 * math.exp(-0.3 * l), F32)
        (akl, bql, bvl, brl, gl, zl, aqt, avt, bkt, zt) = _in_proj(
            x, nw[l], wtok[l], wfeat[l], wgu[l], wgut[l], bg[l], bgt[l])
        ya = _attn(linit, lam_params[l], gq_col[l], gk_row[l], sg_col[l], aqt, akl, avt)
        yb = _gla(bql, bkt, bvl, zl, zt, brl, og_row[l])
        x = _merge(x.reshape(B * S, D), ya.reshape(B * S, A_WIDTH), yb.reshape(B * S, B_WIDTH),
                   gl.reshape(B * S, 2 * D), wb[l], wo[l], nf[l], wu[l], wd[l]).reshape(B, S, D)
    return x
```

```python
import functools
import math

import jax
import jax.numpy as jnp
from jax import lax
from jax.experimental import pallas as pl
from jax.experimental.pallas import tpu as pltpu

D_MODEL = 1024
DEPTH = 4
CHUNK = 64
EPS = 1e-6
A_HEADS = 4
A_QK_DIM = 64
A_V_DIM = 128
A_WIDTH = 512
B_HEADS = 4
B_K_DIM = 64
B_V_DIM = 128
B_WIDTH = 512
B_GATE_RANK = 16
B_GATE_TAU = 16.0
D_FF = 4 * D_MODEL

VMEM_LIMIT_BYTES = 56 * 1024 * 1024
LANES = 128

TM_PROJ = 512
TQ = 256
TK = 256
PAIR = 2 * CHUNK
TG = 512
TM_MERGE = 512
FF_CHUNK = 1024
NEG_BIG = -1e30
LOG2_E = 1.4426950408889634
SAFE_LOG2_SCORE = 100.0

BF16 = jnp.bfloat16
F32 = jnp.float32


def _dot(a, b):
    return jnp.dot(a, b, preferred_element_type=F32)


def _dot_nt(a, b):
    return lax.dot_general(a, b, (((1,), (1,)), ((), ())), preferred_element_type=F32)


def _in_proj_body(x_ref, nw_ref, wtok_ref, wfeat_ref, wgu_ref, wgut_ref, bg_ref, bgt_ref,
                  gq_ref, gk_ref,
                  khat_ref, bq_ref, bv_ref, br_ref, gates_ref, z_ref,
                  qbd_ref, avt_ref, bkt_ref, zt_ref):
    x = x_ref[0]
    ms = jnp.mean(x * x, axis=-1, keepdims=True)
    u = (x * lax.rsqrt(ms + EPS) * nw_ref[...]).astype(BF16)
    n_q = TM_PROJ // TQ

    def tok(start, width):
        return _dot(u, wtok_ref[:, start:start + width])

    ak = tok(0, 512)
    lane = lax.broadcasted_iota(jnp.int32, (TM_PROJ, LANES), 1)
    first_map = lane < A_QK_DIM
    inv_d = 1.0 / A_QK_DIM
    for h in range(A_HEADS):
        kk = ak[:, h * LANES:(h + 1) * LANES]
        k2 = kk * kk
        s1 = jnp.sum(jnp.where(first_map, k2, 0.0), axis=-1, keepdims=True)
        s2 = jnp.sum(jnp.where(first_map, 0.0, k2), axis=-1, keepdims=True)
        r = jnp.where(first_map, lax.rsqrt(s1 * inv_d + EPS), lax.rsqrt(s2 * inv_d + EPS))
        khat = (kk * r * gk_ref[...]).astype(BF16)
        for s in range(n_q):
            khat_ref[0, s, h] = khat[s * TK:(s + 1) * TK]
    bq_ref[0] = tok(512, 256).astype(BF16)
    bv_ref[0] = tok(768, 512).astype(BF16)
    br_ref[0] = tok(1280, 512).astype(BF16)
    for c in range(4):
        gates_ref[0, :, c * 512:(c + 1) * 512] = tok(1792 + c * 512, 512).astype(BF16)
    ba = tok(3840, LANES).astype(BF16)
    z_ref[0] = _dot(ba, wgu_ref[...]) + bg_ref[...]

    def feat(start, rows):
        return _dot_nt(wfeat_ref[start:start + rows, :], u)

    aqt = feat(0, 512)
    zeros = jnp.zeros((A_QK_DIM, TQ), BF16)
    for g in range(2 * A_HEADS):
        h, m = divmod(g, 2)
        qg = aqt[g * A_QK_DIM:(g + 1) * A_QK_DIM]
        rq = lax.rsqrt(jnp.mean(qg * qg, axis=0, keepdims=True) + EPS)
        qhat = (qg * rq * gq_ref[...]).astype(BF16)
        for s in range(n_q):
            rows = slice(m * A_QK_DIM, (m + 1) * A_QK_DIM)
            qbd_ref[0, s, h, rows, m * TQ:(m + 1) * TQ] = qhat[:, s * TQ:(s + 1) * TQ]
            qbd_ref[0, s, h, rows, (1 - m) * TQ:(2 - m) * TQ] = zeros
    avt = feat(512, 512).astype(BF16)
    for s in range(n_q):
        avt_ref[0, s] = avt[:, s * TQ:(s + 1) * TQ]
    bkt = feat(1024, 256).astype(BF16)
    bat = feat(1280, B_GATE_RANK).astype(BF16)
    zt = _dot(wgut_ref[...], bat) + bgt_ref[...]
    for s in range(TM_PROJ // PAIR):
        bkt_ref[0, s] = bkt[:, s * PAIR:(s + 1) * PAIR]
        zt_ref[0, s] = zt[:, s * PAIR:(s + 1) * PAIR]


def _in_proj(x, nw, wtok, wfeat, wgu, wgut, bg, bgt, gq_col, gk_row):
    B, S, D = x.shape
    tm = TM_PROJ
    grid = (B, S // tm)
    const2 = lambda b, i: (0, 0)
    tokspec = lambda w: pl.BlockSpec((1, tm, w), lambda b, i: (b, i, 0))
    out_shape = (
        jax.ShapeDtypeStruct((B, S // TK, A_HEADS, TK, LANES), BF16),
        jax.ShapeDtypeStruct((B, S, 256), BF16),
        jax.ShapeDtypeStruct((B, S, 512), BF16),
        jax.ShapeDtypeStruct((B, S, 512), BF16),
        jax.ShapeDtypeStruct((B, S, 2 * D_MODEL), BF16),
        jax.ShapeDtypeStruct((B, S, 256), F32),
        jax.ShapeDtypeStruct((B, S // TQ, A_HEADS, LANES, 2 * TQ), BF16),
        jax.ShapeDtypeStruct((B, S // TQ, 512, TQ), BF16),
        jax.ShapeDtypeStruct((B, S // PAIR, 256, PAIR), BF16),
        jax.ShapeDtypeStruct((B, S // PAIR, 256, PAIR), F32),
    )
    out_specs = (
        pl.BlockSpec((1, tm // TK, A_HEADS, TK, LANES), lambda b, i: (b, i, 0, 0, 0)),
        tokspec(256), tokspec(512), tokspec(512), tokspec(2 * D_MODEL), tokspec(256),
        pl.BlockSpec((1, tm // TQ, A_HEADS, LANES, 2 * TQ), lambda b, i: (b, i, 0, 0, 0)),
        pl.BlockSpec((1, tm // TQ, 512, TQ), lambda b, i: (b, i, 0, 0)),
        pl.BlockSpec((1, tm // PAIR, 256, PAIR), lambda b, i: (b, i, 0, 0)),
        pl.BlockSpec((1, tm // PAIR, 256, PAIR), lambda b, i: (b, i, 0, 0)),
    )
    in_specs = [
        pl.BlockSpec((1, tm, D), lambda b, i: (b, i, 0)),
        pl.BlockSpec(nw.shape, const2),
        pl.BlockSpec(wtok.shape, const2),
        pl.BlockSpec(wfeat.shape, const2),
        pl.BlockSpec(wgu.shape, const2),
        pl.BlockSpec(wgut.shape, const2),
        pl.BlockSpec(bg.shape, const2),
        pl.BlockSpec(bgt.shape, const2),
        pl.BlockSpec(gq_col.shape, const2),
        pl.BlockSpec(gk_row.shape, const2),
    ]
    return pl.pallas_call(
        _in_proj_body,
        out_shape=out_shape,
        grid=grid,
        in_specs=in_specs,
        out_specs=out_specs,
        compiler_params=pltpu.CompilerParams(
            dimension_semantics=("parallel", "arbitrary"),
            vmem_limit_bytes=VMEM_LIMIT_BYTES),
        name="in_proj",
    )(x, nw, wtok, wfeat, wgu, wgut, bg, bgt, gq_col, gk_row)


def _attn_body(linit_ref, fast_ref, lam_ref, sg_ref, qbd_ref, khat_ref, vt_ref, o_ref,
               m_scr, l_scr, acc_scr):
    i = pl.program_id(1)
    shape = (TK, 2 * TQ)
    kc = lax.shift_right_logical(lax.broadcasted_iota(jnp.int32, shape, 0), 6)
    qc = lax.shift_right_logical(lax.broadcasted_iota(jnp.int32, shape, 1) & (TQ - 1), 6)
    allowed = kc <= qc

    l_scr[...] = jnp.zeros(l_scr.shape, F32)
    acc_scr[...] = jnp.zeros(acc_scr.shape, F32)

    def scores(j, h):
        return _dot(khat_ref[0, j, h], qbd_ref[0, 0, h])

    def values(j, h):
        return vt_ref[0, j, h * LANES:(h + 1) * LANES, :]

    def fast_blocks(blocks):
        ss = [[scores(j, h) for h in range(A_HEADS)] for j, _ in blocks]
        for (j, masked), sj in zip(blocks, ss):
            for h in range(A_HEADS):
                s = jnp.where(allowed, sj[h], NEG_BIG) if masked else sj[h]
                p = jnp.exp2(s)
                l_scr[h] += jnp.sum(p, axis=0, keepdims=True)
                acc_scr[h] += _dot(values(j, h), p.astype(BF16))

    @pl.when(fast_ref[0] != 0)
    def _():
        def pair(jj, carry):
            fast_blocks([(2 * jj, False), (2 * jj + 1, False)])
            return carry
        lax.fori_loop(0, i // 2, pair, 0)

        @pl.when(i % 2 == 1)
        def _():
            fast_blocks([(i - 1, False), (i, True)])

        @pl.when(i % 2 == 0)
        def _():
            fast_blocks([(i, True)])

    @pl.when(fast_ref[0] == 0)
    def _():
        m_scr[...] = jnp.full(m_scr.shape, NEG_BIG, F32)

        def online_block(j, carry):
            keep = jnp.logical_or(allowed, j < i)
            for h in range(A_HEADS):
                s = jnp.where(keep, scores(j, h), NEG_BIG)
                m_old = m_scr[h]
                m_new = jnp.maximum(m_old, jnp.max(s, axis=0, keepdims=True))
                alpha = jnp.exp2(m_old - m_new)
                p = jnp.exp2(s - m_new)
                l_scr[h] = alpha * l_scr[h] + jnp.sum(p, axis=0, keepdims=True)
                m_scr[h] = m_new
                acc_scr[h] = alpha * acc_scr[h] + _dot(values(j, h), p.astype(BF16))
            return carry
        lax.fori_loop(0, i + 1, online_block, 0)

    lp = lam_ref[...]
    d1 = jnp.sum(lp[0:1] * lp[1:2], axis=-1, keepdims=True)
    d2 = jnp.sum(lp[2:3] * lp[3:4], axis=-1, keepdims=True)
    linit = linit_ref[0]
    lam = jnp.exp(d1) - jnp.exp(d2) + linit
    for h in range(A_HEADS):
        inv_l = 1.0 / l_scr[h]
        acc = acc_scr[h]
        o = acc[:, :TQ] * inv_l[:, :TQ] - lam * (acc[:, TQ:] * inv_l[:, TQ:])
        ms = jnp.mean(o * o, axis=0, keepdims=True)
        y = o * lax.rsqrt(ms + EPS) * sg_ref[...] * (1.0 - linit)
        o_ref[0, :, h * LANES:(h + 1) * LANES] = y.T.astype(BF16)


def _attn(linit, fast, lam_params, sg_col, qbd, khat, avt):
    B, n_kv = khat.shape[:2]
    S = n_kv * TK
    n_q = S // TQ
    grid = (B, n_q)
    const2 = lambda b, i: (0, 0)
    in_specs = [
        pl.BlockSpec(memory_space=pltpu.SMEM),
        pl.BlockSpec(memory_space=pltpu.SMEM),
        pl.BlockSpec(lam_params.shape, const2),
        pl.BlockSpec(sg_col.shape, const2),
        pl.BlockSpec((1, 1, A_HEADS, LANES, 2 * TQ), lambda b, i: (b, i, 0, 0, 0)),
        pl.BlockSpec((1, n_kv, A_HEADS, TK, LANES), lambda b, i: (b, 0, 0, 0, 0)),
        pl.BlockSpec((1, n_kv, A_WIDTH, TK), lambda b, i: (b, 0, 0, 0)),
    ]
    return pl.pallas_call(
        _attn_body,
        out_shape=jax.ShapeDtypeStruct((B, S, A_WIDTH), BF16),
        grid=grid,
        in_specs=in_specs,
        out_specs=pl.BlockSpec((1, TQ, A_WIDTH), lambda b, i: (b, i, 0)),
        scratch_shapes=[
            pltpu.VMEM((A_HEADS, 1, 2 * TQ), F32),
            pltpu.VMEM((A_HEADS, 1, 2 * TQ), F32),
            pltpu.VMEM((A_HEADS, LANES, 2 * TQ), F32),
        ],
        compiler_params=pltpu.CompilerParams(
            dimension_semantics=("parallel", "arbitrary"),
            vmem_limit_bytes=VMEM_LIMIT_BYTES),
        name="diff_attn",
    )(linit, fast, lam_params, sg_col, qbd, khat, avt)


def _log_sigmoid(z):
    return jnp.minimum(z, 0.0) - jnp.log(1.0 + jnp.exp(-jnp.abs(z)))


def _split3(g):
    hi = g.astype(BF16)
    r1 = g - hi.astype(F32)
    mid = r1.astype(BF16)
    lo = (r1 - mid.astype(F32)).astype(BF16)
    return hi, mid, lo


def _gla_body(bq_ref, bkt_ref, bv_ref, z_ref, zt_ref, br_ref, og_ref, o_ref, state_scr):
    @pl.when(pl.program_id(1) == 0)
    def _():
        state_scr[...] = jnp.zeros(state_scr.shape, F32)

    r = lax.broadcasted_iota(jnp.int32, (PAIR, PAIR), 0)
    c = lax.broadcasted_iota(jnp.int32, (PAIR, PAIR), 1)
    same = lax.shift_right_logical(r, 6) == lax.shift_right_logical(c, 6)
    tril_mask = same & (c <= r)
    tril = jnp.where(tril_mask, 1.0, 0.0).astype(BF16)
    triu = jnp.where(same & (r <= c), 1.0, 0.0).astype(BF16)
    lane_q = lax.broadcasted_iota(jnp.int32, (PAIR, B_HEADS * B_K_DIM), 1)
    lane_t = lax.broadcasted_iota(jnp.int32, (B_HEADS * B_K_DIM, PAIR), 1)
    first_chunk = lane_t < CHUNK
    inv_tau = 1.0 / B_GATE_TAU

    def pair_step(p, carry):
        rows = pl.ds(pl.multiple_of(p * PAIR, PAIR), PAIR)
        g = _log_sigmoid(z_ref[0, rows, :]) * inv_tau
        gt = _log_sigmoid(zt_ref[0, p]) * inv_tau
        g_hi, g_mid, g_lo = _split3(g)
        G = _dot(tril, g_hi) + _dot(tril, g_mid) + _dot(tril, g_lo)
        t_hi, t_mid, t_lo = _split3(gt)
        Gt = _dot(t_hi, triu) + _dot(t_mid, triu) + _dot(t_lo, triu)
        gl0 = jnp.sum(jnp.where(first_chunk, gt, 0.0), axis=-1, keepdims=True)
        gl1 = jnp.sum(jnp.where(first_chunk, 0.0, gt), axis=-1, keepdims=True)
        glast = jnp.where(first_chunk, gl0, gl1)

        q_dec = (bq_ref[0, rows, :].astype(F32) * (B_K_DIM ** -0.5) * jnp.exp(G)).astype(BF16)
        kt = bkt_ref[0, p].astype(F32)
        kdt = (kt * jnp.exp(-Gt)).astype(BF16)
        kst = kt * jnp.exp(glast - Gt)
        kst0 = jnp.where(first_chunk, kst, 0.0).astype(BF16)
        kst1 = jnp.where(first_chunk, 0.0, kst).astype(BF16)
        v = bv_ref[0, rows, :]

        s0 = state_scr[...]
        kv0 = jnp.concatenate(
            [_dot(kst0[h * B_K_DIM:(h + 1) * B_K_DIM], v[:, h * B_V_DIM:(h + 1) * B_V_DIM])
             for h in range(B_HEADS)], axis=0)
        kv1 = jnp.concatenate(
            [_dot(kst1[h * B_K_DIM:(h + 1) * B_K_DIM], v[:, h * B_V_DIM:(h + 1) * B_V_DIM])
             for h in range(B_HEADS)], axis=0)
        s1 = jnp.exp(gl0) * s0 + kv0
        s2 = jnp.exp(gl1) * s1 + kv1
        state_scr[...] = s2
        s0b = s0.astype(BF16)
        s1b = s1.astype(BF16)

        og = og_ref[...]
        for h in range(B_HEADS):
            head = (lane_q >= h * B_K_DIM) & (lane_q < (h + 1) * B_K_DIM)
            qh = jnp.where(head, q_dec, jnp.zeros_like(q_dec))
            a = _dot(qh, kdt)
            a = jnp.where(tril_mask, a, 0.0).astype(BF16)
            vh = v[:, h * B_V_DIM:(h + 1) * B_V_DIM]
            o_intra = _dot(a, vh)
            o_inter = jnp.concatenate(
                [_dot(qh[:CHUNK], s0b), _dot(qh[CHUNK:], s1b)], axis=0)
            o = o_intra + o_inter
            ms = jnp.mean(o * o, axis=-1, keepdims=True)
            rr = br_ref[0, rows, h * B_V_DIM:(h + 1) * B_V_DIM].astype(F32)
            silu = rr / (1.0 + jnp.exp(-rr))
            o_ref[0, rows, h * B_V_DIM:(h + 1) * B_V_DIM] = (
                o * lax.rsqrt(ms + EPS) * og * silu).astype(BF16)
        return carry

    lax.fori_loop(0, TG // PAIR, pair_step, 0)


def _gla(bq, bkt, bv, z, zt, br, og):
    B, S, _ = bq.shape
    grid = (B, S // TG)
    tokspec = lambda w: pl.BlockSpec((1, TG, w), lambda b, i: (b, i, 0))
    featspec = pl.BlockSpec((1, TG // PAIR, 256, PAIR), lambda b, i: (b, i, 0, 0))
    return pl.pallas_call(
        _gla_body,
        out_shape=jax.ShapeDtypeStruct((B, S, B_WIDTH), BF16),
        grid=grid,
        in_specs=[tokspec(256), featspec, tokspec(512), tokspec(256), featspec, tokspec(512),
                  pl.BlockSpec(og.shape, lambda b, i: (0, 0))],
        out_specs=tokspec(512),
        scratch_shapes=[pltpu.VMEM((B_HEADS * B_K_DIM, B_V_DIM), F32)],
        compiler_params=pltpu.CompilerParams(
            dimension_semantics=("parallel", "arbitrary"),
            vmem_limit_bytes=VMEM_LIMIT_BYTES),
        name="gla",
    )(bq, bkt, bv, z, zt, br, og)


def _merge_body(x_ref, ya_ref, yb_ref, g_ref, wb_ref, wo_ref, nf_ref, wu_ref, wd_ref, o_ref):
    pa = _dot(ya_ref[...], wb_ref[0])
    pb = _dot(yb_ref[...], wb_ref[1])
    ga = 1.0 / (1.0 + jnp.exp(-g_ref[:, :D_MODEL].astype(F32)))
    gb = 1.0 / (1.0 + jnp.exp(-g_ref[:, D_MODEL:].astype(F32)))
    mixed = (ga * pa + gb * pb).astype(BF16)
    x1 = x_ref[...] + _dot(mixed, wo_ref[...])
    ms = jnp.mean(x1 * x1, axis=-1, keepdims=True)
    h = (x1 * lax.rsqrt(ms + EPS) * nf_ref[...]).astype(BF16)
    acc = x1
    for c in range(D_FF // FF_CHUNK):
        up = _dot(h, wu_ref[:, c * FF_CHUNK:(c + 1) * FF_CHUNK])
        act = jnp.square(jnp.maximum(up, 0.0)).astype(BF16)
        acc = acc + _dot(act, wd_ref[c * FF_CHUNK:(c + 1) * FF_CHUNK, :])
    o_ref[...] = acc


def _merge(x2, ya2, yb2, gates2, wb, wo, nf, wu, wd):
    T, D = x2.shape
    tm = TM_MERGE
    row = lambda w: pl.BlockSpec((tm, w), lambda i: (i, 0))
    const = lambda a: pl.BlockSpec(a.shape, lambda i: (0,) * a.ndim,
                                   pipeline_mode=pl.Buffered(1))
    return pl.pallas_call(
        _merge_body,
        out_shape=jax.ShapeDtypeStruct((T, D), F32),
        grid=(T // tm,),
        in_specs=[row(D), row(A_WIDTH), row(B_WIDTH), row(2 * D),
                  const(wb), const(wo), const(nf), const(wu), const(wd)],
        out_specs=row(D),
        compiler_params=pltpu.CompilerParams(
            dimension_semantics=("parallel",),
            vmem_limit_bytes=VMEM_LIMIT_BYTES),
        name="merge_mlp",
    )(x2, ya2, yb2, gates2, wb, wo, nf, wu, wd)


def _split_w_in(w_in):
    widths = (512, 512, 512, 256, 256, 512, B_GATE_RANK, 512, 2 * D_MODEL)
    parts, start = [], 0
    for w in widths:
        parts.append(w_in[..., start:start + w])
        start += w
    return parts


def kernel(x, norm_mix, w_in, a_q_norm, a_k_norm, a_lambda_q1, a_lambda_k1, a_lambda_q2,
           a_lambda_k2, a_sub_norm, b_gate_up, b_gate_bias, b_out_norm, w_branch, w_out,
           norm_ffn, w_up, w_down):
    B, S, D = x.shape
    L = DEPTH
    aq, ak, av, bq, bk, bv, ba, br, gates = _split_w_in(w_in)
    ba_pad = jnp.pad(ba, ((0, 0), (0, 0), (0, LANES - B_GATE_RANK)))
    wtok = jnp.concatenate([ak, bq, bv, br, gates, ba_pad], axis=-1).astype(BF16)
    wfeat = jnp.swapaxes(jnp.concatenate([aq, av, bk, ba], axis=-1), 1, 2).astype(BF16)
    wgu = jnp.pad(b_gate_up, ((0, 0), (0, LANES - B_GATE_RANK), (0, 0))).astype(BF16)
    wgut = jnp.swapaxes(b_gate_up, 1, 2).astype(BF16)
    bg = b_gate_bias[:, None, :]
    bgt = b_gate_bias[:, :, None]
    nw = norm_mix[:, None, :]
    nf = norm_ffn[:, None, :]
    q_fold = A_QK_DIM ** -0.5 * LOG2_E
    gq_col = (a_q_norm * q_fold)[:, :, None]
    gk_row = jnp.tile(a_k_norm, (1, 2))[:, None, :]
    score_bound = (A_QK_DIM * q_fold * jnp.max(jnp.abs(a_q_norm), axis=1)
                   * jnp.max(jnp.abs(a_k_norm), axis=1))
    fast = (score_bound <= SAFE_LOG2_SCORE).astype(jnp.int32)[:, None]
    sg_col = a_sub_norm[:, :, None]
    og_row = b_out_norm[:, None, :]
    lam_params = jnp.stack([a_lambda_q1, a_lambda_k1, a_lambda_q2, a_lambda_k2], axis=1)
    wb = w_branch.astype(BF16)
    wo = w_out.astype(BF16)
    wu = w_up.astype(BF16)
    wd = w_down.astype(BF16)

    for l in range(L):
        linit = jnp.full((1,), 0.8 - 0.6 * math.exp(-0.3 * l), F32)
        (khat, bql, bvl, brl, gl, zl, qbd, avt, bkt, zt) = _in_proj(
            x, nw[l], wtok[l], wfeat[l], wgu[l], wgut[l], bg[l], bgt[l], gq_col[l], gk_row[l])
        ya = _attn(linit, fast[l], lam_params[l], sg_col[l], qbd, khat, avt)
        yb = _gla(bql, bkt, bvl, zl, zt, brl, og_row[l])
        x = _merge(x.reshape(B * S, D), ya.reshape(B * S, A_WIDTH), yb.reshape(B * S, B_WIDTH),
                   gl.reshape(B * S, 2 * D), wb[l], wo[l], nf[l], wu[l], wd[l]).reshape(B, S, D)
    return x
```

```python
import functools
import math

import jax
import jax.numpy as jnp
from jax import lax
from jax.experimental import pallas as pl
from jax.experimental.pallas import tpu as pltpu

D_MODEL = 1024
DEPTH = 4
CHUNK = 64
EPS = 1e-6
A_HEADS = 4
A_QK_DIM = 64
A_V_DIM = 128
A_WIDTH = 512
B_HEADS = 4
B_K_DIM = 64
B_V_DIM = 128
B_WIDTH = 512
B_GATE_RANK = 16
B_GATE_TAU = 16.0
D_FF = 4 * D_MODEL

VMEM_LIMIT_BYTES = 56 * 1024 * 1024
LANES = 128

TM_PROJ = 512
TQ = 256
TK = 256
PAIR = 2 * CHUNK
TG = 512
TM_MERGE = 512
FF_CHUNK = 1024
NEG_BIG = -1e30
LOG2_E = 1.4426950408889634
SAFE_LOG2_SCORE = 100.0

BF16 = jnp.bfloat16
F32 = jnp.float32


def _dot(a, b):
    return jnp.dot(a, b, preferred_element_type=F32)


def _dot_nt(a, b):
    return lax.dot_general(a, b, (((1,), (1,)), ((), ())), preferred_element_type=F32)


def _sigmoid(x):
    return 0.5 * jnp.tanh(0.5 * x) + 0.5


def _log_sigmoid(z):
    return jnp.minimum(z, 0.0) - jnp.log(1.0 + jnp.exp(-jnp.abs(z)))


def _split2(g):
    hi = g.astype(BF16)
    lo = (g - hi.astype(F32)).astype(BF16)
    return hi, lo


def _in_proj_body(x_ref, nw_ref, wtok_ref, wfeat_ref, wgu_ref, bg_ref, gq_ref, gk_ref,
                  khat_ref, bq_ref, bv_ref, br_ref, gates_ref, gcum_ref,
                  qbd_ref, avt_ref, bkt_ref):
    x = x_ref[0]
    ms = jnp.mean(x * x, axis=-1, keepdims=True)
    u = (x * lax.rsqrt(ms + EPS) * nw_ref[...]).astype(BF16)
    n_q = TM_PROJ // TQ

    def tok(start, width):
        return _dot(u, wtok_ref[:, start:start + width])

    ak = tok(0, 512)
    lane = lax.broadcasted_iota(jnp.int32, (TM_PROJ, LANES), 1)
    first_map = lane < A_QK_DIM
    inv_d = 1.0 / A_QK_DIM
    for h in range(A_HEADS):
        kk = ak[:, h * LANES:(h + 1) * LANES]
        k2 = kk * kk
        s1 = jnp.sum(jnp.where(first_map, k2, 0.0), axis=-1, keepdims=True)
        s2 = jnp.sum(jnp.where(first_map, 0.0, k2), axis=-1, keepdims=True)
        r = jnp.where(first_map, lax.rsqrt(s1 * inv_d + EPS), lax.rsqrt(s2 * inv_d + EPS))
        khat = (kk * r * gk_ref[...]).astype(BF16)
        for s in range(n_q):
            khat_ref[0, s, h] = khat[s * TK:(s + 1) * TK]
    bq_ref[0] = tok(512, 256).astype(BF16)
    bv_ref[0] = tok(768, 512).astype(BF16)
    br_ref[0] = tok(1280, 512).astype(BF16)
    for c in range(4):
        gates_ref[0, :, c * 512:(c + 1) * 512] = tok(1792 + c * 512, 512).astype(BF16)
    ba = tok(3840, LANES).astype(BF16)
    z = _dot(ba, wgu_ref[...]) + bg_ref[...]
    g_hi, g_lo = _split2(_log_sigmoid(z) * (1.0 / B_GATE_TAU))
    r = lax.broadcasted_iota(jnp.int32, (PAIR, PAIR), 0)
    c = lax.broadcasted_iota(jnp.int32, (PAIR, PAIR), 1)
    same_chunk = lax.shift_right_logical(r, 6) == lax.shift_right_logical(c, 6)
    tril = jnp.where(same_chunk & (c <= r), 1.0, 0.0).astype(BF16)
    for s in range(TM_PROJ // PAIR):
        rows = slice(s * PAIR, (s + 1) * PAIR)
        gcum_ref[0, rows, :] = _dot(tril, g_hi[rows]) + _dot(tril, g_lo[rows])

    def feat(start, rows):
        return _dot_nt(wfeat_ref[start:start + rows, :], u)

    aqt = feat(0, 512)
    zeros = jnp.zeros((A_QK_DIM, TQ), BF16)
    for g in range(2 * A_HEADS):
        h, m = divmod(g, 2)
        qg = aqt[g * A_QK_DIM:(g + 1) * A_QK_DIM]
        rq = lax.rsqrt(jnp.mean(qg * qg, axis=0, keepdims=True) + EPS)
        qhat = (qg * rq * gq_ref[...]).astype(BF16)
        for s in range(n_q):
            rows = slice(m * A_QK_DIM, (m + 1) * A_QK_DIM)
            qbd_ref[0, s, h, rows, m * TQ:(m + 1) * TQ] = qhat[:, s * TQ:(s + 1) * TQ]
            qbd_ref[0, s, h, rows, (1 - m) * TQ:(2 - m) * TQ] = zeros
    avt = feat(512, 512).astype(BF16)
    for s in range(n_q):
        avt_ref[0, s] = avt[:, s * TQ:(s + 1) * TQ]
    bkt = feat(1024, 256).astype(BF16)
    for s in range(TM_PROJ // PAIR):
        bkt_ref[0, s] = bkt[:, s * PAIR:(s + 1) * PAIR]


def _in_proj(x, nw, wtok, wfeat, wgu, bg, gq_col, gk_row):
    B, S, D = x.shape
    tm = TM_PROJ
    grid = (B, S // tm)
    const2 = lambda b, i: (0, 0)
    tokspec = lambda w: pl.BlockSpec((1, tm, w), lambda b, i: (b, i, 0))
    out_shape = (
        jax.ShapeDtypeStruct((B, S // TK, A_HEADS, TK, LANES), BF16),
        jax.ShapeDtypeStruct((B, S, 256), BF16),
        jax.ShapeDtypeStruct((B, S, 512), BF16),
        jax.ShapeDtypeStruct((B, S, 512), BF16),
        jax.ShapeDtypeStruct((B, S, 2 * D_MODEL), BF16),
        jax.ShapeDtypeStruct((B, S, 256), F32),
        jax.ShapeDtypeStruct((B, S // TQ, A_HEADS, LANES, 2 * TQ), BF16),
        jax.ShapeDtypeStruct((B, S // TQ, 512, TQ), BF16),
        jax.ShapeDtypeStruct((B, S // PAIR, 256, PAIR), BF16),
    )
    out_specs = (
        pl.BlockSpec((1, tm // TK, A_HEADS, TK, LANES), lambda b, i: (b, i, 0, 0, 0)),
        tokspec(256), tokspec(512), tokspec(512), tokspec(2 * D_MODEL), tokspec(256),
        pl.BlockSpec((1, tm // TQ, A_HEADS, LANES, 2 * TQ), lambda b, i: (b, i, 0, 0, 0)),
        pl.BlockSpec((1, tm // TQ, 512, TQ), lambda b, i: (b, i, 0, 0)),
        pl.BlockSpec((1, tm // PAIR, 256, PAIR), lambda b, i: (b, i, 0, 0)),
    )
    in_specs = [
        pl.BlockSpec((1, tm, D), lambda b, i: (b, i, 0)),
        pl.BlockSpec(nw.shape, const2),
        pl.BlockSpec(wtok.shape, const2),
        pl.BlockSpec(wfeat.shape, const2),
        pl.BlockSpec(wgu.shape, const2),
        pl.BlockSpec(bg.shape, const2),
        pl.BlockSpec(gq_col.shape, const2),
        pl.BlockSpec(gk_row.shape, const2),
    ]
    return pl.pallas_call(
        _in_proj_body,
        out_shape=out_shape,
        grid=grid,
        in_specs=in_specs,
        out_specs=out_specs,
        compiler_params=pltpu.CompilerParams(
            dimension_semantics=("parallel", "arbitrary"),
            vmem_limit_bytes=VMEM_LIMIT_BYTES),
        name="in_proj",
    )(x, nw, wtok, wfeat, wgu, bg, gq_col, gk_row)


def _attn_body(linit_ref, fast_ref, lam_ref, sg_ref, qbd_ref, khat_ref, vt_ref, o_ref,
               m_scr, l_scr, acc_scr):
    i = pl.program_id(1)
    shape = (TK, 2 * TQ)
    kc = lax.shift_right_logical(lax.broadcasted_iota(jnp.int32, shape, 0), 6)
    qc = lax.shift_right_logical(lax.broadcasted_iota(jnp.int32, shape, 1) & (TQ - 1), 6)
    allowed = kc <= qc

    l_scr[...] = jnp.zeros(l_scr.shape, F32)
    acc_scr[...] = jnp.zeros(acc_scr.shape, F32)

    def scores(j, h):
        return _dot(khat_ref[0, j, h], qbd_ref[0, 0, h])

    def values(j, h):
        return vt_ref[0, j, h * LANES:(h + 1) * LANES, :]

    def fast_blocks(blocks):
        ss = [[scores(j, h) for h in range(A_HEADS)] for j, _ in blocks]
        for (j, masked), sj in zip(blocks, ss):
            for h in range(A_HEADS):
                s = jnp.where(allowed, sj[h], NEG_BIG) if masked else sj[h]
                p = jnp.exp2(s)
                l_scr[h] += jnp.sum(p, axis=0, keepdims=True)
                acc_scr[h] += _dot(values(j, h), p.astype(BF16))

    @pl.when(fast_ref[0] != 0)
    def _():
        def pair(jj, carry):
            fast_blocks([(2 * jj, False), (2 * jj + 1, False)])
            return carry
        lax.fori_loop(0, i // 2, pair, 0)

        @pl.when(i % 2 == 1)
        def _():
            fast_blocks([(i - 1, False), (i, True)])

        @pl.when(i % 2 == 0)
        def _():
            fast_blocks([(i, True)])

    @pl.when(fast_ref[0] == 0)
    def _():
        m_scr[...] = jnp.full(m_scr.shape, NEG_BIG, F32)

        def online_block(j, carry):
            keep = jnp.logical_or(allowed, j < i)
            for h in range(A_HEADS):
                s = jnp.where(keep, scores(j, h), NEG_BIG)
                m_old = m_scr[h]
                m_new = jnp.maximum(m_old, jnp.max(s, axis=0, keepdims=True))
                alpha = jnp.exp2(m_old - m_new)
                p = jnp.exp2(s - m_new)
                l_scr[h] = alpha * l_scr[h] + jnp.sum(p, axis=0, keepdims=True)
                m_scr[h] = m_new
                acc_scr[h] = alpha * acc_scr[h] + _dot(values(j, h), p.astype(BF16))
            return carry
        lax.fori_loop(0, i + 1, online_block, 0)

    lp = lam_ref[...]
    d1 = jnp.sum(lp[0:1] * lp[1:2], axis=-1, keepdims=True)
    d2 = jnp.sum(lp[2:3] * lp[3:4], axis=-1, keepdims=True)
    linit = linit_ref[0]
    lam = jnp.exp(d1) - jnp.exp(d2) + linit
    for h in range(A_HEADS):
        inv_l = 1.0 / l_scr[h]
        acc = acc_scr[h]
        o = acc[:, :TQ] * inv_l[:, :TQ] - lam * (acc[:, TQ:] * inv_l[:, TQ:])
        ms = jnp.mean(o * o, axis=0, keepdims=True)
        y = o * lax.rsqrt(ms + EPS) * sg_ref[...] * (1.0 - linit)
        o_ref[0, :, h * LANES:(h + 1) * LANES] = y.T.astype(BF16)


def _attn(linit, fast, lam_params, sg_col, qbd, khat, avt):
    B, n_kv = khat.shape[:2]
    S = n_kv * TK
    n_q = S // TQ
    grid = (B, n_q)
    const2 = lambda b, i: (0, 0)
    in_specs = [
        pl.BlockSpec(memory_space=pltpu.SMEM),
        pl.BlockSpec(memory_space=pltpu.SMEM),
        pl.BlockSpec(lam_params.shape, const2),
        pl.BlockSpec(sg_col.shape, const2),
        pl.BlockSpec((1, 1, A_HEADS, LANES, 2 * TQ), lambda b, i: (b, i, 0, 0, 0)),
        pl.BlockSpec((1, n_kv, A_HEADS, TK, LANES), lambda b, i: (b, 0, 0, 0, 0)),
        pl.BlockSpec((1, n_kv, A_WIDTH, TK), lambda b, i: (b, 0, 0, 0)),
    ]
    return pl.pallas_call(
        _attn_body,
        out_shape=jax.ShapeDtypeStruct((B, S, A_WIDTH), BF16),
        grid=grid,
        in_specs=in_specs,
        out_specs=pl.BlockSpec((1, TQ, A_WIDTH), lambda b, i: (b, i, 0)),
        scratch_shapes=[
            pltpu.VMEM((A_HEADS, 1, 2 * TQ), F32),
            pltpu.VMEM((A_HEADS, 1, 2 * TQ), F32),
            pltpu.VMEM((A_HEADS, LANES, 2 * TQ), F32),
        ],
        compiler_params=pltpu.CompilerParams(
            dimension_semantics=("parallel", "arbitrary"),
            vmem_limit_bytes=VMEM_LIMIT_BYTES),
        name="diff_attn",
    )(linit, fast, lam_params, sg_col, qbd, khat, avt)


def _gla_body(bq_ref, bkt_ref, bv_ref, gcum_ref, o_ref, state_scr):
    @pl.when(pl.program_id(1) == 0)
    def _():
        state_scr[...] = jnp.zeros(state_scr.shape, F32)

    n_pair = TG // PAIR
    hdk = B_HEADS * B_K_DIM
    r = lax.broadcasted_iota(jnp.int32, (PAIR, PAIR), 0)
    c = lax.broadcasted_iota(jnp.int32, (PAIR, PAIR), 1)
    same = lax.shift_right_logical(r, 6) == lax.shift_right_logical(c, 6)
    tril_mask = same & (c <= r)
    lane_q = lax.broadcasted_iota(jnp.int32, (PAIR, hdk), 1)
    first_chunk = lax.broadcasted_iota(jnp.int32, (hdk, PAIR), 1) < CHUNK
    heads = [(lane_q >= h * B_K_DIM) & (lane_q < (h + 1) * B_K_DIM) for h in range(B_HEADS)]
    rows = [slice(p * PAIR, (p + 1) * PAIR) for p in range(n_pair)]
    vcol = [slice(h * B_V_DIM, (h + 1) * B_V_DIM) for h in range(B_HEADS)]
    krow = [slice(h * B_K_DIM, (h + 1) * B_K_DIM) for h in range(B_HEADS)]

    q_dec, kdt, kst, decay = [], [], [], []
    for p in range(n_pair):
        Gp = gcum_ref[0, rows[p], :]
        q_dec.append((bq_ref[0, rows[p], :].astype(F32) * (B_K_DIM ** -0.5)
                      * jnp.exp(Gp)).astype(BF16))
        Gt = Gp.T
        gl0 = jnp.broadcast_to(Gp[CHUNK - 1:CHUNK, :], (PAIR, hdk)).T
        gl1 = jnp.broadcast_to(Gp[PAIR - 1:PAIR, :], (PAIR, hdk)).T
        kd = bkt_ref[0, p].astype(F32) * jnp.exp(-Gt)
        ks = kd * jnp.exp(jnp.where(first_chunk, gl0, gl1))
        kdt.append(kd.astype(BF16))
        kst.append((jnp.where(first_chunk, ks, 0.0).astype(BF16),
                    jnp.where(first_chunk, 0.0, ks).astype(BF16)))
        decay.append((jnp.exp(gl0), jnp.exp(gl1)))

    qh = [[jnp.where(heads[h], q_dec[p], jnp.zeros_like(q_dec[p])) for h in range(B_HEADS)]
          for p in range(n_pair)]
    a = [[_dot(qh[p][h], kdt[p]) for h in range(B_HEADS)] for p in range(n_pair)]
    kv = [[jnp.concatenate([_dot(kst[p][c][krow[h]], bv_ref[0, rows[p], vcol[h]])
                            for h in range(B_HEADS)], axis=0)
           for c in range(2)] for p in range(n_pair)]

    states = [state_scr[...]]
    for p in range(n_pair):
        for c in range(2):
            states.append(decay[p][c] * states[-1] + kv[p][c])
    state_scr[...] = states[-1]
    sb = [s.astype(BF16) for s in states[:-1]]

    for p in range(n_pair):
        for h in range(B_HEADS):
            am = jnp.where(tril_mask, a[p][h], 0.0).astype(BF16)
            o = _dot(am, bv_ref[0, rows[p], vcol[h]]) + jnp.concatenate(
                [_dot(qh[p][h][:CHUNK], sb[2 * p]), _dot(qh[p][h][CHUNK:], sb[2 * p + 1])], axis=0)
            o_ref[0, rows[p], vcol[h]] = o.astype(BF16)


def _gla(bq, bkt, bv, gcum):
    B, S, _ = bq.shape
    grid = (B, S // TG)
    tokspec = lambda w: pl.BlockSpec((1, TG, w), lambda b, i: (b, i, 0))
    featspec = pl.BlockSpec((1, TG // PAIR, 256, PAIR), lambda b, i: (b, i, 0, 0))
    return pl.pallas_call(
        _gla_body,
        out_shape=jax.ShapeDtypeStruct((B, S, B_WIDTH), BF16),
        grid=grid,
        in_specs=[tokspec(256), featspec, tokspec(512), tokspec(256)],
        out_specs=tokspec(512),
        scratch_shapes=[pltpu.VMEM((B_HEADS * B_K_DIM, B_V_DIM), F32)],
        compiler_params=pltpu.CompilerParams(
            dimension_semantics=("parallel", "arbitrary"),
            vmem_limit_bytes=VMEM_LIMIT_BYTES),
        name="gla",
    )(bq, bkt, bv, gcum)


def _merge_body(x_ref, ya_ref, ob_ref, br_ref, g_ref, og_ref, wb_ref, wo_ref, nf_ref, wu_ref,
                wd_ref, o_ref):
    og = og_ref[...]
    yb = []
    for h in range(B_HEADS):
        cols = slice(h * B_V_DIM, (h + 1) * B_V_DIM)
        o = ob_ref[:, cols].astype(F32)
        rr = br_ref[:, cols].astype(F32)
        ms_o = jnp.mean(o * o, axis=-1, keepdims=True)
        yb.append((o * lax.rsqrt(ms_o + EPS) * og * (rr * _sigmoid(rr))).astype(BF16))
    pa = _dot(ya_ref[...], wb_ref[0])
    pb = _dot(jnp.concatenate(yb, axis=-1), wb_ref[1])
    ga = _sigmoid(g_ref[:, :D_MODEL].astype(F32))
    gb = _sigmoid(g_ref[:, D_MODEL:].astype(F32))
    mixed = (ga * pa + gb * pb).astype(BF16)
    x1 = x_ref[...] + _dot(mixed, wo_ref[...])
    ms = jnp.mean(x1 * x1, axis=-1, keepdims=True)
    h = (x1 * lax.rsqrt(ms + EPS) * nf_ref[...]).astype(BF16)
    acc = x1
    for c in range(D_FF // FF_CHUNK):
        up = _dot(h, wu_ref[:, c * FF_CHUNK:(c + 1) * FF_CHUNK])
        act = jnp.square(jnp.maximum(up, 0.0)).astype(BF16)
        acc = acc + _dot(act, wd_ref[c * FF_CHUNK:(c + 1) * FF_CHUNK, :])
    o_ref[...] = acc


def _merge(x2, ya2, ob2, br2, gates2, og, wb, wo, nf, wu, wd):
    T, D = x2.shape
    tm = TM_MERGE
    row = lambda w: pl.BlockSpec((tm, w), lambda i: (i, 0))
    const = lambda a: pl.BlockSpec(a.shape, lambda i: (0,) * a.ndim,
                                   pipeline_mode=pl.Buffered(1))
    return pl.pallas_call(
        _merge_body,
        out_shape=jax.ShapeDtypeStruct((T, D), F32),
        grid=(T // tm,),
        in_specs=[row(D), row(A_WIDTH), row(B_WIDTH), row(B_WIDTH), row(2 * D), const(og),
                  const(wb), const(wo), const(nf), const(wu), const(wd)],
        out_specs=row(D),
        compiler_params=pltpu.CompilerParams(
            dimension_semantics=("parallel",),
            vmem_limit_bytes=VMEM_LIMIT_BYTES),
        name="merge_mlp",
    )(x2, ya2, ob2, br2, gates2, og, wb, wo, nf, wu, wd)


def _split_w_in(w_in):
    widths = (512, 512, 512, 256, 256, 512, B_GATE_RANK, 512, 2 * D_MODEL)
    parts, start = [], 0
    for w in widths:
        parts.append(w_in[..., start:start + w])
        start += w
    return parts


def kernel(x, norm_mix, w_in, a_q_norm, a_k_norm, a_lambda_q1, a_lambda_k1, a_lambda_q2,
           a_lambda_k2, a_sub_norm, b_gate_up, b_gate_bias, b_out_norm, w_branch, w_out,
           norm_ffn, w_up, w_down):
    B, S, D = x.shape
    L = DEPTH
    aq, ak, av, bq, bk, bv, ba, br, gates = _split_w_in(w_in)
    ba_pad = jnp.pad(ba, ((0, 0), (0, 0), (0, LANES - B_GATE_RANK)))
    wtok = jnp.concatenate([ak, bq, bv, br, gates, ba_pad], axis=-1).astype(BF16)
    wfeat = jnp.swapaxes(jnp.concatenate([aq, av, bk], axis=-1), 1, 2).astype(BF16)
    wgu = jnp.pad(b_gate_up, ((0, 0), (0, LANES - B_GATE_RANK), (0, 0))).astype(BF16)
    bg = b_gate_bias[:, None, :]
    nw = norm_mix[:, None, :]
    nf = norm_ffn[:, None, :]
    q_fold = A_QK_DIM ** -0.5 * LOG2_E
    gq_col = (a_q_norm * q_fold)[:, :, None]
    gk_row = jnp.tile(a_k_norm, (1, 2))[:, None, :]
    score_bound = (A_QK_DIM * q_fold * jnp.max(jnp.abs(a_q_norm), axis=1)
                   * jnp.max(jnp.abs(a_k_norm), axis=1))
    fast = (score_bound <= SAFE_LOG2_SCORE).astype(jnp.int32)[:, None]
    sg_col = a_sub_norm[:, :, None]
    og_row = b_out_norm[:, None, :]
    lam_params = jnp.stack([a_lambda_q1, a_lambda_k1, a_lambda_q2, a_lambda_k2], axis=1)
    wb = w_branch.astype(BF16)
    wo = w_out.astype(BF16)
    wu = w_up.astype(BF16)
    wd = w_down.astype(BF16)

    for l in range(L):
        linit = jnp.full((1,), 0.8 - 0.6 * math.exp(-0.3 * l), F32)
        (khat, bql, bvl, brl, gl, gcum, qbd, avt, bkt) = _in_proj(
            x, nw[l], wtok[l], wfeat[l], wgu[l], bg[l], gq_col[l], gk_row[l])
        ya = _attn(linit, fast[l], lam_params[l], sg_col[l], qbd, khat, avt)
        ob = _gla(bql, bkt, bvl, gcum)
        x = _merge(x.reshape(B * S, D), ya.reshape(B * S, A_WIDTH), ob.reshape(B * S, B_WIDTH),
                   brl.reshape(B * S, B_WIDTH), gl.reshape(B * S, 2 * D), og_row[l],
                   wb[l], wo[l], nf[l], wu[l], wd[l]).reshape(B, S, D)
    return x
```

```python
import functools
import math

import jax
import jax.numpy as jnp
from jax import lax
from jax.experimental import pallas as pl
from jax.experimental.pallas import tpu as pltpu

D_MODEL = 1024
DEPTH = 4
CHUNK = 64
EPS = 1e-6
A_HEADS = 4
A_QK_DIM = 64
A_V_DIM = 128
A_WIDTH = 512
B_HEADS = 4
B_K_DIM = 64
B_V_DIM = 128
B_WIDTH = 512
B_GATE_RANK = 16
B_GATE_TAU = 16.0
D_FF = 4 * D_MODEL

VMEM_LIMIT_BYTES = 56 * 1024 * 1024
LANES = 128

TM_PROJ = 512
TQ = 256
TK = 256
KV_PER_ITER = 3
PAIR = 2 * CHUNK
TG = 512
TM_MERGE = 512
FF_CHUNK = 1024
NEG_BIG = -1e30
LOG2_E = 1.4426950408889634
SAFE_LOG2_SCORE = 100.0

BF16 = jnp.bfloat16
F32 = jnp.float32


def _dot(a, b):
    return jnp.dot(a, b, preferred_element_type=F32)


def _dot_nt(a, b):
    return lax.dot_general(a, b, (((1,), (1,)), ((), ())), preferred_element_type=F32)


def _sigmoid(x):
    return 0.5 * jnp.tanh(0.5 * x) + 0.5


def _log_sigmoid(z):
    return jnp.minimum(z, 0.0) - jnp.log(1.0 + jnp.exp(-jnp.abs(z)))


def _split2(g):
    hi = g.astype(BF16)
    lo = (g - hi.astype(F32)).astype(BF16)
    return hi, lo


def _in_proj_body(x_ref, nw_ref, wtok_ref, wfeat_ref, wgu_ref, bg_ref, gq_ref, gk_ref,
                  khat_ref, bq_ref, bv_ref, br_ref, gates_ref, gcum_ref,
                  qbd_ref, avt_ref, bkt_ref):
    x = x_ref[0]
    ms = jnp.mean(x * x, axis=-1, keepdims=True)
    u = (x * lax.rsqrt(ms + EPS) * nw_ref[...]).astype(BF16)
    n_q = TM_PROJ // TQ

    def tok(start, width):
        return _dot(u, wtok_ref[:, start:start + width])

    ak = tok(0, 512)
    lane = lax.broadcasted_iota(jnp.int32, (TM_PROJ, LANES), 1)
    first_map = lane < A_QK_DIM
    inv_d = 1.0 / A_QK_DIM
    for h in range(A_HEADS):
        kk = ak[:, h * LANES:(h + 1) * LANES]
        k2 = kk * kk
        s1 = jnp.sum(jnp.where(first_map, k2, 0.0), axis=-1, keepdims=True)
        s2 = jnp.sum(jnp.where(first_map, 0.0, k2), axis=-1, keepdims=True)
        r = jnp.where(first_map, lax.rsqrt(s1 * inv_d + EPS), lax.rsqrt(s2 * inv_d + EPS))
        khat = (kk * r * gk_ref[...]).astype(BF16)
        for s in range(n_q):
            khat_ref[0, s, h] = khat[s * TK:(s + 1) * TK]
    bq_ref[0] = tok(512, 256).astype(BF16)
    bv_ref[0] = tok(768, 512).astype(BF16)
    br_ref[0] = tok(1280, 512).astype(BF16)
    for c in range(4):
        gates_ref[0, :, c * 512:(c + 1) * 512] = tok(1792 + c * 512, 512).astype(BF16)
    ba = tok(3840, LANES).astype(BF16)
    z = _dot(ba, wgu_ref[...]) + bg_ref[...]
    g_hi, g_lo = _split2(_log_sigmoid(z) * (1.0 / B_GATE_TAU))
    r = lax.broadcasted_iota(jnp.int32, (PAIR, PAIR), 0)
    c = lax.broadcasted_iota(jnp.int32, (PAIR, PAIR), 1)
    same_chunk = lax.shift_right_logical(r, 6) == lax.shift_right_logical(c, 6)
    tril = jnp.where(same_chunk & (c <= r), 1.0, 0.0).astype(BF16)
    for s in range(TM_PROJ // PAIR):
        rows = slice(s * PAIR, (s + 1) * PAIR)
        gcum_ref[0, rows, :] = _dot(tril, g_hi[rows]) + _dot(tril, g_lo[rows])

    def feat(start, rows):
        return _dot_nt(wfeat_ref[start:start + rows, :], u)

    aqt = feat(0, 512)
    zeros = jnp.zeros((A_QK_DIM, TQ), BF16)
    for g in range(2 * A_HEADS):
        h, m = divmod(g, 2)
        qg = aqt[g * A_QK_DIM:(g + 1) * A_QK_DIM]
        rq = lax.rsqrt(jnp.mean(qg * qg, axis=0, keepdims=True) + EPS)
        qhat = (qg * rq * gq_ref[...]).astype(BF16)
        for s in range(n_q):
            rows = slice(m * A_QK_DIM, (m + 1) * A_QK_DIM)
            qbd_ref[0, s, h, rows, m * TQ:(m + 1) * TQ] = qhat[:, s * TQ:(s + 1) * TQ]
            qbd_ref[0, s, h, rows, (1 - m) * TQ:(2 - m) * TQ] = zeros
    avt = feat(512, 512).astype(BF16)
    for s in range(n_q):
        avt_ref[0, s] = avt[:, s * TQ:(s + 1) * TQ]
    bkt = feat(1024, 256).astype(BF16)
    for s in range(TM_PROJ // PAIR):
        bkt_ref[0, s] = bkt[:, s * PAIR:(s + 1) * PAIR]


def _in_proj(x, nw, wtok, wfeat, wgu, bg, gq_col, gk_row):
    B, S, D = x.shape
    tm = TM_PROJ
    grid = (B, S // tm)
    const2 = lambda b, i: (0, 0)
    tokspec = lambda w: pl.BlockSpec((1, tm, w), lambda b, i: (b, i, 0))
    out_shape = (
        jax.ShapeDtypeStruct((B, S // TK, A_HEADS, TK, LANES), BF16),
        jax.ShapeDtypeStruct((B, S, 256), BF16),
        jax.ShapeDtypeStruct((B, S, 512), BF16),
        jax.ShapeDtypeStruct((B, S, 512), BF16),
        jax.ShapeDtypeStruct((B, S, 2 * D_MODEL), BF16),
        jax.ShapeDtypeStruct((B, S, 256), F32),
        jax.ShapeDtypeStruct((B, S // TQ, A_HEADS, LANES, 2 * TQ), BF16),
        jax.ShapeDtypeStruct((B, S // TQ, 512, TQ), BF16),
        jax.ShapeDtypeStruct((B, S // PAIR, 256, PAIR), BF16),
    )
    out_specs = (
        pl.BlockSpec((1, tm // TK, A_HEADS, TK, LANES), lambda b, i: (b, i, 0, 0, 0)),
        tokspec(256), tokspec(512), tokspec(512), tokspec(2 * D_MODEL), tokspec(256),
        pl.BlockSpec((1, tm // TQ, A_HEADS, LANES, 2 * TQ), lambda b, i: (b, i, 0, 0, 0)),
        pl.BlockSpec((1, tm // TQ, 512, TQ), lambda b, i: (b, i, 0, 0)),
        pl.BlockSpec((1, tm // PAIR, 256, PAIR), lambda b, i: (b, i, 0, 0)),
    )
    in_specs = [
        pl.BlockSpec((1, tm, D), lambda b, i: (b, i, 0)),
        pl.BlockSpec(nw.shape, const2),
        pl.BlockSpec(wtok.shape, const2),
        pl.BlockSpec(wfeat.shape, const2),
        pl.BlockSpec(wgu.shape, const2),
        pl.BlockSpec(bg.shape, const2),
        pl.BlockSpec(gq_col.shape, const2),
        pl.BlockSpec(gk_row.shape, const2),
    ]
    return pl.pallas_call(
        _in_proj_body,
        out_shape=out_shape,
        grid=grid,
        in_specs=in_specs,
        out_specs=out_specs,
        compiler_params=pltpu.CompilerParams(
            dimension_semantics=("parallel", "arbitrary"),
            vmem_limit_bytes=VMEM_LIMIT_BYTES),
        name="in_proj",
    )(x, nw, wtok, wfeat, wgu, bg, gq_col, gk_row)


def _attn_body(linit_ref, fast_ref, lam_ref, sg_ref, qbd_ref, khat_ref, vt_ref, o_ref,
               m_scr, l_scr, acc_scr):
    i = pl.program_id(1)
    shape = (TK, 2 * TQ)
    kc = lax.shift_right_logical(lax.broadcasted_iota(jnp.int32, shape, 0), 6)
    qc = lax.shift_right_logical(lax.broadcasted_iota(jnp.int32, shape, 1) & (TQ - 1), 6)
    allowed = kc <= qc

    l_scr[...] = jnp.zeros(l_scr.shape, F32)
    acc_scr[...] = jnp.zeros(acc_scr.shape, F32)

    def scores(j, h):
        return _dot(khat_ref[0, j, h], qbd_ref[0, 0, h])

    def values(j, h):
        return vt_ref[0, j, h * LANES:(h + 1) * LANES, :]

    def fast_blocks(blocks):
        ss = [[scores(j, h) for h in range(A_HEADS)] for j, _ in blocks]
        for (j, masked), sj in zip(blocks, ss):
            for h in range(A_HEADS):
                s = jnp.where(allowed, sj[h], NEG_BIG) if masked else sj[h]
                p = jnp.exp2(s)
                l_scr[h] += jnp.sum(p, axis=0, keepdims=True)
                acc_scr[h] += _dot(values(j, h), p.astype(BF16))

    @pl.when(fast_ref[0] != 0)
    def _():
        n_groups = i // KV_PER_ITER

        def group(jj, carry):
            fast_blocks([(KV_PER_ITER * jj + k, False) for k in range(KV_PER_ITER)])
            return carry
        lax.fori_loop(0, n_groups, group, 0)

        left = i - KV_PER_ITER * n_groups
        for n_left in range(KV_PER_ITER):
            @pl.when(left == n_left)
            def _(n_left=n_left):
                fast_blocks([(i - n_left + k, False) for k in range(n_left)] + [(i, True)])

    @pl.when(fast_ref[0] == 0)
    def _():
        m_scr[...] = jnp.full(m_scr.shape, NEG_BIG, F32)

        def online_block(j, carry):
            keep = jnp.logical_or(allowed, j < i)
            for h in range(A_HEADS):
                s = jnp.where(keep, scores(j, h), NEG_BIG)
                m_old = m_scr[h]
                m_new = jnp.maximum(m_old, jnp.max(s, axis=0, keepdims=True))
                alpha = jnp.exp2(m_old - m_new)
                p = jnp.exp2(s - m_new)
                l_scr[h] = alpha * l_scr[h] + jnp.sum(p, axis=0, keepdims=True)
                m_scr[h] = m_new
                acc_scr[h] = alpha * acc_scr[h] + _dot(values(j, h), p.astype(BF16))
            return carry
        lax.fori_loop(0, i + 1, online_block, 0)

    lp = lam_ref[...]
    d1 = jnp.sum(lp[0:1] * lp[1:2], axis=-1, keepdims=True)
    d2 = jnp.sum(lp[2:3] * lp[3:4], axis=-1, keepdims=True)
    linit = linit_ref[0]
    lam = jnp.exp(d1) - jnp.exp(d2) + linit
    for h in range(A_HEADS):
        inv_l = 1.0 / l_scr[h]
        acc = acc_scr[h]
        o = acc[:, :TQ] * inv_l[:, :TQ] - lam * (acc[:, TQ:] * inv_l[:, TQ:])
        ms = jnp.mean(o * o, axis=0, keepdims=True)
        y = o * lax.rsqrt(ms + EPS) * sg_ref[...] * (1.0 - linit)
        o_ref[0, :, h * LANES:(h + 1) * LANES] = y.T.astype(BF16)


def _attn(linit, fast, lam_params, sg_col, qbd, khat, avt):
    B, n_kv = khat.shape[:2]
    S = n_kv * TK
    n_q = S // TQ
    grid = (B, n_q)
    const2 = lambda b, i: (0, 0)
    in_specs = [
        pl.BlockSpec(memory_space=pltpu.SMEM),
        pl.BlockSpec(memory_space=pltpu.SMEM),
        pl.BlockSpec(lam_params.shape, const2),
        pl.BlockSpec(sg_col.shape, const2),
        pl.BlockSpec((1, 1, A_HEADS, LANES, 2 * TQ), lambda b, i: (b, i, 0, 0, 0)),
        pl.BlockSpec((1, n_kv, A_HEADS, TK, LANES), lambda b, i: (b, 0, 0, 0, 0)),
        pl.BlockSpec((1, n_kv, A_WIDTH, TK), lambda b, i: (b, 0, 0, 0)),
    ]
    return pl.pallas_call(
        _attn_body,
        out_shape=jax.ShapeDtypeStruct((B, S, A_WIDTH), BF16),
        grid=grid,
        in_specs=in_specs,
        out_specs=pl.BlockSpec((1, TQ, A_WIDTH), lambda b, i: (b, i, 0)),
        scratch_shapes=[
            pltpu.VMEM((A_HEADS, 1, 2 * TQ), F32),
            pltpu.VMEM((A_HEADS, 1, 2 * TQ), F32),
            pltpu.VMEM((A_HEADS, LANES, 2 * TQ), F32),
        ],
        compiler_params=pltpu.CompilerParams(
            dimension_semantics=("parallel", "arbitrary"),
            vmem_limit_bytes=VMEM_LIMIT_BYTES),
        name="diff_attn",
    )(linit, fast, lam_params, sg_col, qbd, khat, avt)


def _gla_body(bq_ref, bkt_ref, bv_ref, gcum_ref, o_ref, state_scr):
    @pl.when(pl.program_id(1) == 0)
    def _():
        state_scr[...] = jnp.zeros(state_scr.shape, F32)

    n_pair = TG // PAIR
    hdk = B_HEADS * B_K_DIM
    r = lax.broadcasted_iota(jnp.int32, (PAIR, PAIR), 0)
    c = lax.broadcasted_iota(jnp.int32, (PAIR, PAIR), 1)
    same = lax.shift_right_logical(r, 6) == lax.shift_right_logical(c, 6)
    tril_mask = same & (c <= r)
    lane_q = lax.broadcasted_iota(jnp.int32, (PAIR, hdk), 1)
    first_chunk = lax.broadcasted_iota(jnp.int32, (hdk, PAIR), 1) < CHUNK
    heads = [(lane_q >= h * B_K_DIM) & (lane_q < (h + 1) * B_K_DIM) for h in range(B_HEADS)]
    rows = [slice(p * PAIR, (p + 1) * PAIR) for p in range(n_pair)]
    vcol = [slice(h * B_V_DIM, (h + 1) * B_V_DIM) for h in range(B_HEADS)]
    krow = [slice(h * B_K_DIM, (h + 1) * B_K_DIM) for h in range(B_HEADS)]

    q_dec, kdt, kst, decay = [], [], [], []
    for p in range(n_pair):
        Gp = gcum_ref[0, rows[p], :]
        q_dec.append((bq_ref[0, rows[p], :].astype(F32) * (B_K_DIM ** -0.5)
                      * jnp.exp(Gp)).astype(BF16))
        Gt = Gp.T
        gl0 = jnp.broadcast_to(Gp[CHUNK - 1:CHUNK, :], (PAIR, hdk)).T
        gl1 = jnp.broadcast_to(Gp[PAIR - 1:PAIR, :], (PAIR, hdk)).T
        kd = bkt_ref[0, p].astype(F32) * jnp.exp(-Gt)
        ks = kd * jnp.exp(jnp.where(first_chunk, gl0, gl1))
        kdt.append(kd.astype(BF16))
        kst.append((jnp.where(first_chunk, ks, 0.0).astype(BF16),
                    jnp.where(first_chunk, 0.0, ks).astype(BF16)))
        decay.append((jnp.exp(gl0), jnp.exp(gl1)))

    qh = [[jnp.where(heads[h], q_dec[p], jnp.zeros_like(q_dec[p])) for h in range(B_HEADS)]
          for p in range(n_pair)]
    a = [[_dot(qh[p][h], kdt[p]) for h in range(B_HEADS)] for p in range(n_pair)]
    kv = [[jnp.concatenate([_dot(kst[p][c][krow[h]], bv_ref[0, rows[p], vcol[h]])
                            for h in range(B_HEADS)], axis=0)
           for c in range(2)] for p in range(n_pair)]

    states = [state_scr[...]]
    for p in range(n_pair):
        for c in range(2):
            states.append(decay[p][c] * states[-1] + kv[p][c])
    state_scr[...] = states[-1]
    sb = [s.astype(BF16) for s in states[:-1]]

    for p in range(n_pair):
        for h in range(B_HEADS):
            am = jnp.where(tril_mask, a[p][h], 0.0).astype(BF16)
            o = _dot(am, bv_ref[0, rows[p], vcol[h]]) + jnp.concatenate(
                [_dot(qh[p][h][:CHUNK], sb[2 * p]), _dot(qh[p][h][CHUNK:], sb[2 * p + 1])], axis=0)
            o_ref[0, rows[p], vcol[h]] = o.astype(BF16)


def _gla(bq, bkt, bv, gcum):
    B, S, _ = bq.shape
    grid = (B, S // TG)
    tokspec = lambda w: pl.BlockSpec((1, TG, w), lambda b, i: (b, i, 0))
    featspec = pl.BlockSpec((1, TG // PAIR, 256, PAIR), lambda b, i: (b, i, 0, 0))
    return pl.pallas_call(
        _gla_body,
        out_shape=jax.ShapeDtypeStruct((B, S, B_WIDTH), BF16),
        grid=grid,
        in_specs=[tokspec(256), featspec, tokspec(512), tokspec(256)],
        out_specs=tokspec(512),
        scratch_shapes=[pltpu.VMEM((B_HEADS * B_K_DIM, B_V_DIM), F32)],
        compiler_params=pltpu.CompilerParams(
            dimension_semantics=("parallel", "arbitrary"),
            vmem_limit_bytes=VMEM_LIMIT_BYTES),
        name="gla",
    )(bq, bkt, bv, gcum)


def _merge_body(x_ref, ya_ref, ob_ref, br_ref, g_ref, og_ref, wb_ref, wo_ref, nf_ref, wu_ref,
                wd_ref, o_ref):
    og = og_ref[...]
    yb = []
    for h in range(B_HEADS):
        cols = slice(h * B_V_DIM, (h + 1) * B_V_DIM)
        o = ob_ref[:, cols].astype(F32)
        rr = br_ref[:, cols].astype(F32)
        ms_o = jnp.mean(o * o, axis=-1, keepdims=True)
        yb.append((o * lax.rsqrt(ms_o + EPS) * og * (rr * _sigmoid(rr))).astype(BF16))
    pa = _dot(ya_ref[...], wb_ref[0])
    pb = _dot(jnp.concatenate(yb, axis=-1), wb_ref[1])
    ga = _sigmoid(g_ref[:, :D_MODEL].astype(F32))
    gb = _sigmoid(g_ref[:, D_MODEL:].astype(F32))
    mixed = (ga * pa + gb * pb).astype(BF16)
    x1 = x_ref[...] + _dot(mixed, wo_ref[...])
    ms = jnp.mean(x1 * x1, axis=-1, keepdims=True)
    h = (x1 * lax.rsqrt(ms + EPS) * nf_ref[...]).astype(BF16)
    acc = x1
    for c in range(D_FF // FF_CHUNK):
        up = _dot(h, wu_ref[:, c * FF_CHUNK:(c + 1) * FF_CHUNK])
        act = jnp.square(jnp.maximum(up, 0.0)).astype(BF16)
        acc = acc + _dot(act, wd_ref[c * FF_CHUNK:(c + 1) * FF_CHUNK, :])
    o_ref[...] = acc


def _merge(x2, ya2, ob2, br2, gates2, og, wb, wo, nf, wu, wd):
    T, D = x2.shape
    tm = TM_MERGE
    row = lambda w: pl.BlockSpec((tm, w), lambda i: (i, 0))
    const = lambda a: pl.BlockSpec(a.shape, lambda i: (0,) * a.ndim,
                                   pipeline_mode=pl.Buffered(1))
    return pl.pallas_call(
        _merge_body,
        out_shape=jax.ShapeDtypeStruct((T, D), F32),
        grid=(T // tm,),
        in_specs=[row(D), row(A_WIDTH), row(B_WIDTH), row(B_WIDTH), row(2 * D), const(og),
                  const(wb), const(wo), const(nf), const(wu), const(wd)],
        out_specs=row(D),
        compiler_params=pltpu.CompilerParams(
            dimension_semantics=("parallel",),
            vmem_limit_bytes=VMEM_LIMIT_BYTES),
        name="merge_mlp",
    )(x2, ya2, ob2, br2, gates2, og, wb, wo, nf, wu, wd)


def _split_w_in(w_in):
    widths = (512, 512, 512, 256, 256, 512, B_GATE_RANK, 512, 2 * D_MODEL)
    parts, start = [], 0
    for w in widths:
        parts.append(w_in[..., start:start + w])
        start += w
    return parts


def kernel(x, norm_mix, w_in, a_q_norm, a_k_norm, a_lambda_q1, a_lambda_k1, a_lambda_q2,
           a_lambda_k2, a_sub_norm, b_gate_up, b_gate_bias, b_out_norm, w_branch, w_out,
           norm_ffn, w_up, w_down):
    B, S, D = x.shape
    L = DEPTH
    aq, ak, av, bq, bk, bv, ba, br, gates = _split_w_in(w_in.astype(BF16))
    ba_pad = jnp.pad(ba, ((0, 0), (0, 0), (0, LANES - B_GATE_RANK)))
    wtok = jnp.concatenate([ak, bq, bv, br, gates, ba_pad], axis=-1)
    wfeat = jnp.swapaxes(jnp.concatenate([aq, av, bk], axis=-1), 1, 2)
    wgu = jnp.pad(b_gate_up, ((0, 0), (0, LANES - B_GATE_RANK), (0, 0))).astype(BF16)
    bg = b_gate_bias[:, None, :]
    nw = norm_mix[:, None, :]
    nf = norm_ffn[:, None, :]
    q_fold = A_QK_DIM ** -0.5 * LOG2_E
    gq_col = (a_q_norm * q_fold)[:, :, None]
    gk_row = jnp.tile(a_k_norm, (1, 2))[:, None, :]
    score_bound = (A_QK_DIM * q_fold * jnp.max(jnp.abs(a_q_norm), axis=1)
                   * jnp.max(jnp.abs(a_k_norm), axis=1))
    fast = (score_bound <= SAFE_LOG2_SCORE).astype(jnp.int32)[:, None]
    sg_col = a_sub_norm[:, :, None]
    og_row = b_out_norm[:, None, :]
    lam_params = jnp.stack([a_lambda_q1, a_lambda_k1, a_lambda_q2, a_lambda_k2], axis=1)
    wb = w_branch.astype(BF16)
    wo = w_out.astype(BF16)
    wu = w_up.astype(BF16)
    wd = w_down.astype(BF16)

    for l in range(L):
        linit = jnp.full((1,), 0.8 - 0.6 * math.exp(-0.3 * l), F32)
        (khat, bql, bvl, brl, gl, gcum, qbd, avt, bkt) = _in_proj(
            x, nw[l], wtok[l], wfeat[l], wgu[l], bg[l], gq_col[l], gk_row[l])
        ya = _attn(linit, fast[l], lam_params[l], sg_col[l], qbd, khat, avt)
        ob = _gla(bql, bkt, bvl, gcum)
        x = _merge(x.reshape(B * S, D), ya.reshape(B * S, A_WIDTH), ob.reshape(B * S, B_WIDTH),
                   brl.reshape(B * S, B_WIDTH), gl.reshape(B * S, 2 * D), og_row[l],
                   wb[l], wo[l], nf[l], wu[l], wd[l]).reshape(B, S, D)
    return x
```

```python
import functools
import math

import jax
import jax.numpy as jnp
from jax import lax
from jax.experimental import pallas as pl
from jax.experimental.pallas import tpu as pltpu

D_MODEL = 1024
DEPTH = 4
CHUNK = 64
EPS = 1e-6
A_HEADS = 4
A_QK_DIM = 64
A_V_DIM = 128
A_WIDTH = 512
B_HEADS = 4
B_K_DIM = 64
B_V_DIM = 128
B_WIDTH = 512
B_GATE_RANK = 16
B_GATE_TAU = 16.0
D_FF = 4 * D_MODEL

VMEM_LIMIT_BYTES = 56 * 1024 * 1024
LANES = 128

TM_PROJ = 512
TQ = 256
TK = 256
KV_PER_ITER = 3
PAIR = 2 * CHUNK
TG = 512
TM_MERGE = 512
FF_CHUNK = 1024
NEG_BIG = -1e30
LOG2_E = 1.4426950408889634
SAFE_LOG2_SCORE = 100.0

BF16 = jnp.bfloat16
F32 = jnp.float32


def _dot(a, b):
    return jnp.dot(a, b, preferred_element_type=F32)


def _dot_nt(a, b):
    return lax.dot_general(a, b, (((1,), (1,)), ((), ())), preferred_element_type=F32)


def _sigmoid(x):
    return 0.5 * jnp.tanh(0.5 * x) + 0.5


def _log_sigmoid(z):
    return jnp.minimum(z, 0.0) - jnp.log(1.0 + jnp.exp(-jnp.abs(z)))


def _split2(g):
    hi = g.astype(BF16)
    lo = (g - hi.astype(F32)).astype(BF16)
    return hi, lo


def _in_proj_body(x_ref, nw_ref, wtok_ref, wfeat_ref, wgu_ref, bg_ref, gq_ref, gk_ref,
                  khat_ref, bq_ref, bv_ref, br_ref, gates_ref, gcum_ref,
                  qbd_ref, avt_ref, bkt_ref):
    x = x_ref[0]
    ms = jnp.mean(x * x, axis=-1, keepdims=True)
    u = (x * lax.rsqrt(ms + EPS) * nw_ref[...]).astype(BF16)
    n_q = TM_PROJ // TQ

    def tok(start, width):
        return _dot(u, wtok_ref[:, start:start + width])

    ak = tok(0, 512)
    lane = lax.broadcasted_iota(jnp.int32, (TM_PROJ, LANES), 1)
    first_map = lane < A_QK_DIM
    inv_d = 1.0 / A_QK_DIM
    for h in range(A_HEADS):
        kk = ak[:, h * LANES:(h + 1) * LANES]
        k2 = kk * kk
        s1 = jnp.sum(jnp.where(first_map, k2, 0.0), axis=-1, keepdims=True)
        s2 = jnp.sum(jnp.where(first_map, 0.0, k2), axis=-1, keepdims=True)
        r = jnp.where(first_map, lax.rsqrt(s1 * inv_d + EPS), lax.rsqrt(s2 * inv_d + EPS))
        khat = (kk * r * gk_ref[...]).astype(BF16)
        for s in range(n_q):
            khat_ref[0, s, h] = khat[s * TK:(s + 1) * TK]
    bq_ref[0] = tok(512, 256).astype(BF16)
    bv_ref[0] = tok(768, 512).astype(BF16)
    br = tok(1280, 512)
    br_ref[0] = (br * _sigmoid(br)).astype(BF16)
    for c in range(4):
        gates_ref[0, :, c * 512:(c + 1) * 512] = _sigmoid(tok(1792 + c * 512, 512)).astype(BF16)
    ba = tok(3840, LANES).astype(BF16)
    z = _dot(ba, wgu_ref[...]) + bg_ref[...]
    g_hi, g_lo = _split2(_log_sigmoid(z) * (1.0 / B_GATE_TAU))
    r = lax.broadcasted_iota(jnp.int32, (PAIR, PAIR), 0)
    c = lax.broadcasted_iota(jnp.int32, (PAIR, PAIR), 1)
    same_chunk = lax.shift_right_logical(r, 6) == lax.shift_right_logical(c, 6)
    tril = jnp.where(same_chunk & (c <= r), 1.0, 0.0).astype(BF16)
    for s in range(TM_PROJ // PAIR):
        rows = slice(s * PAIR, (s + 1) * PAIR)
        gcum_ref[0, rows, :] = _dot(tril, g_hi[rows]) + _dot(tril, g_lo[rows])

    def feat(start, rows):
        return _dot_nt(wfeat_ref[start:start + rows, :], u)

    aqt = feat(0, 512)
    zeros = jnp.zeros((A_QK_DIM, TQ), BF16)
    for g in range(2 * A_HEADS):
        h, m = divmod(g, 2)
        qg = aqt[g * A_QK_DIM:(g + 1) * A_QK_DIM]
        rq = lax.rsqrt(jnp.mean(qg * qg, axis=0, keepdims=True) + EPS)
        qhat = (qg * rq * gq_ref[...]).astype(BF16)
        for s in range(n_q):
            rows = slice(m * A_QK_DIM, (m + 1) * A_QK_DIM)
            qbd_ref[0, s, h, rows, m * TQ:(m + 1) * TQ] = qhat[:, s * TQ:(s + 1) * TQ]
            qbd_ref[0, s, h, rows, (1 - m) * TQ:(2 - m) * TQ] = zeros
    avt = feat(512, 512).astype(BF16)
    for s in range(n_q):
        avt_ref[0, s] = avt[:, s * TQ:(s + 1) * TQ]
    bkt = feat(1024, 256).astype(BF16)
    for s in range(TM_PROJ // PAIR):
        bkt_ref[0, s] = bkt[:, s * PAIR:(s + 1) * PAIR]


def _in_proj(x, nw, wtok, wfeat, wgu, bg, gq_col, gk_row):
    B, S, D = x.shape
    tm = TM_PROJ
    grid = (B, S // tm)
    const2 = lambda b, i: (0, 0)
    tokspec = lambda w: pl.BlockSpec((1, tm, w), lambda b, i: (b, i, 0))
    out_shape = (
        jax.ShapeDtypeStruct((B, S // TK, A_HEADS, TK, LANES), BF16),
        jax.ShapeDtypeStruct((B, S, 256), BF16),
        jax.ShapeDtypeStruct((B, S, 512), BF16),
        jax.ShapeDtypeStruct((B, S, 512), BF16),
        jax.ShapeDtypeStruct((B, S, 2 * D_MODEL), BF16),
        jax.ShapeDtypeStruct((B, S, 256), F32),
        jax.ShapeDtypeStruct((B, S // TQ, A_HEADS, LANES, 2 * TQ), BF16),
        jax.ShapeDtypeStruct((B, S // TQ, 512, TQ), BF16),
        jax.ShapeDtypeStruct((B, S // PAIR, 256, PAIR), BF16),
    )
    out_specs = (
        pl.BlockSpec((1, tm // TK, A_HEADS, TK, LANES), lambda b, i: (b, i, 0, 0, 0)),
        tokspec(256), tokspec(512), tokspec(512), tokspec(2 * D_MODEL), tokspec(256),
        pl.BlockSpec((1, tm // TQ, A_HEADS, LANES, 2 * TQ), lambda b, i: (b, i, 0, 0, 0)),
        pl.BlockSpec((1, tm // TQ, 512, TQ), lambda b, i: (b, i, 0, 0)),
        pl.BlockSpec((1, tm // PAIR, 256, PAIR), lambda b, i: (b, i, 0, 0)),
    )
    in_specs = [
        pl.BlockSpec((1, tm, D), lambda b, i: (b, i, 0)),
        pl.BlockSpec(nw.shape, const2),
        pl.BlockSpec(wtok.shape, const2),
        pl.BlockSpec(wfeat.shape, const2),
        pl.BlockSpec(wgu.shape, const2),
        pl.BlockSpec(bg.shape, const2),
        pl.BlockSpec(gq_col.shape, const2),
        pl.BlockSpec(gk_row.shape, const2),
    ]
    return pl.pallas_call(
        _in_proj_body,
        out_shape=out_shape,
        grid=grid,
        in_specs=in_specs,
        out_specs=out_specs,
        compiler_params=pltpu.CompilerParams(
            dimension_semantics=("parallel", "arbitrary"),
            vmem_limit_bytes=VMEM_LIMIT_BYTES),
        name="in_proj",
    )(x, nw, wtok, wfeat, wgu, bg, gq_col, gk_row)


def _attn_body(linit_ref, fast_ref, lam_ref, sg_ref, qbd_ref, khat_ref, vt_ref, o_ref,
               m_scr, l_scr, acc_scr):
    i = pl.program_id(1)
    shape = (TK, 2 * TQ)
    kc = lax.shift_right_logical(lax.broadcasted_iota(jnp.int32, shape, 0), 6)
    qc = lax.shift_right_logical(lax.broadcasted_iota(jnp.int32, shape, 1) & (TQ - 1), 6)
    allowed = kc <= qc

    l_scr[...] = jnp.zeros(l_scr.shape, F32)
    acc_scr[...] = jnp.zeros(acc_scr.shape, F32)

    def scores(j, h):
        return _dot(khat_ref[0, j, h], qbd_ref[0, 0, h])

    def values(j, h):
        return vt_ref[0, j, h * LANES:(h + 1) * LANES, :]

    def fast_blocks(blocks):
        ss = [[scores(j, h) for h in range(A_HEADS)] for j, _ in blocks]
        for (j, masked), sj in zip(blocks, ss):
            for h in range(A_HEADS):
                s = jnp.where(allowed, sj[h], NEG_BIG) if masked else sj[h]
                p = jnp.exp2(s)
                l_scr[h] += jnp.sum(p, axis=0, keepdims=True)
                acc_scr[h] += _dot(values(j, h), p.astype(BF16))

    @pl.when(fast_ref[0] != 0)
    def _():
        n_groups = i // KV_PER_ITER

        def group(jj, carry):
            fast_blocks([(KV_PER_ITER * jj + k, False) for k in range(KV_PER_ITER)])
            return carry
        lax.fori_loop(0, n_groups, group, 0)

        left = i - KV_PER_ITER * n_groups
        for n_left in range(KV_PER_ITER):
            @pl.when(left == n_left)
            def _(n_left=n_left):
                fast_blocks([(i - n_left + k, False) for k in range(n_left)] + [(i, True)])

    @pl.when(fast_ref[0] == 0)
    def _():
        m_scr[...] = jnp.full(m_scr.shape, NEG_BIG, F32)

        def online_block(j, carry):
            keep = jnp.logical_or(allowed, j < i)
            for h in range(A_HEADS):
                s = jnp.where(keep, scores(j, h), NEG_BIG)
                m_old = m_scr[h]
                m_new = jnp.maximum(m_old, jnp.max(s, axis=0, keepdims=True))
                alpha = jnp.exp2(m_old - m_new)
                p = jnp.exp2(s - m_new)
                l_scr[h] = alpha * l_scr[h] + jnp.sum(p, axis=0, keepdims=True)
                m_scr[h] = m_new
                acc_scr[h] = alpha * acc_scr[h] + _dot(values(j, h), p.astype(BF16))
            return carry
        lax.fori_loop(0, i + 1, online_block, 0)

    lp = lam_ref[...]
    d1 = jnp.sum(lp[0:1] * lp[1:2], axis=-1, keepdims=True)
    d2 = jnp.sum(lp[2:3] * lp[3:4], axis=-1, keepdims=True)
    linit = linit_ref[0]
    lam = jnp.exp(d1) - jnp.exp(d2) + linit
    for h in range(A_HEADS):
        inv_l = 1.0 / l_scr[h]
        acc = acc_scr[h]
        o = acc[:, :TQ] * inv_l[:, :TQ] - lam * (acc[:, TQ:] * inv_l[:, TQ:])
        ms = jnp.mean(o * o, axis=0, keepdims=True)
        y = o * lax.rsqrt(ms + EPS) * sg_ref[...] * (1.0 - linit)
        o_ref[0, :, h * LANES:(h + 1) * LANES] = y.T.astype(BF16)


def _attn(linit, fast, lam_params, sg_col, qbd, khat, avt):
    B, n_kv = khat.shape[:2]
    S = n_kv * TK
    n_q = S // TQ
    grid = (B, n_q)
    const2 = lambda b, i: (0, 0)
    in_specs = [
        pl.BlockSpec(memory_space=pltpu.SMEM),
        pl.BlockSpec(memory_space=pltpu.SMEM),
        pl.BlockSpec(lam_params.shape, const2),
        pl.BlockSpec(sg_col.shape, const2),
        pl.BlockSpec((1, 1, A_HEADS, LANES, 2 * TQ), lambda b, i: (b, i, 0, 0, 0)),
        pl.BlockSpec((1, n_kv, A_HEADS, TK, LANES), lambda b, i: (b, 0, 0, 0, 0)),
        pl.BlockSpec((1, n_kv, A_WIDTH, TK), lambda b, i: (b, 0, 0, 0)),
    ]
    return pl.pallas_call(
        _attn_body,
        out_shape=jax.ShapeDtypeStruct((B, S, A_WIDTH), BF16),
        grid=grid,
        in_specs=in_specs,
        out_specs=pl.BlockSpec((1, TQ, A_WIDTH), lambda b, i: (b, i, 0)),
        scratch_shapes=[
            pltpu.VMEM((A_HEADS, 1, 2 * TQ), F32),
            pltpu.VMEM((A_HEADS, 1, 2 * TQ), F32),
            pltpu.VMEM((A_HEADS, LANES, 2 * TQ), F32),
        ],
        compiler_params=pltpu.CompilerParams(
            dimension_semantics=("parallel", "arbitrary"),
            vmem_limit_bytes=VMEM_LIMIT_BYTES),
        name="diff_attn",
    )(linit, fast, lam_params, sg_col, qbd, khat, avt)


def _gla_body(bq_ref, bkt_ref, bv_ref, gcum_ref, o_ref, state_scr):
    @pl.when(pl.program_id(1) == 0)
    def _():
        state_scr[...] = jnp.zeros(state_scr.shape, F32)

    n_pair = TG // PAIR
    hdk = B_HEADS * B_K_DIM
    r = lax.broadcasted_iota(jnp.int32, (PAIR, PAIR), 0)
    c = lax.broadcasted_iota(jnp.int32, (PAIR, PAIR), 1)
    same = lax.shift_right_logical(r, 6) == lax.shift_right_logical(c, 6)
    tril_mask = same & (c <= r)
    lane_q = lax.broadcasted_iota(jnp.int32, (PAIR, hdk), 1)
    first_chunk = lax.broadcasted_iota(jnp.int32, (hdk, PAIR), 1) < CHUNK
    heads = [(lane_q >= h * B_K_DIM) & (lane_q < (h + 1) * B_K_DIM) for h in range(B_HEADS)]
    rows = [slice(p * PAIR, (p + 1) * PAIR) for p in range(n_pair)]
    vcol = [slice(h * B_V_DIM, (h + 1) * B_V_DIM) for h in range(B_HEADS)]
    krow = [slice(h * B_K_DIM, (h + 1) * B_K_DIM) for h in range(B_HEADS)]

    q_dec, kdt, kst, decay = [], [], [], []
    for p in range(n_pair):
        Gp = gcum_ref[0, rows[p], :]
        q_dec.append((bq_ref[0, rows[p], :].astype(F32) * (B_K_DIM ** -0.5)
                      * jnp.exp(Gp)).astype(BF16))
        Gt = Gp.T
        gl0 = jnp.broadcast_to(Gp[CHUNK - 1:CHUNK, :], (PAIR, hdk)).T
        gl1 = jnp.broadcast_to(Gp[PAIR - 1:PAIR, :], (PAIR, hdk)).T
        kd = bkt_ref[0, p].astype(F32) * jnp.exp(-Gt)
        ks = kd * jnp.exp(jnp.where(first_chunk, gl0, gl1))
        kdt.append(kd.astype(BF16))
        kst.append((jnp.where(first_chunk, ks, 0.0).astype(BF16),
                    jnp.where(first_chunk, 0.0, ks).astype(BF16)))
        decay.append((jnp.exp(gl0), jnp.exp(gl1)))

    qh = [[jnp.where(heads[h], q_dec[p], jnp.zeros_like(q_dec[p])) for h in range(B_HEADS)]
          for p in range(n_pair)]
    a = [[_dot(qh[p][h], kdt[p]) for h in range(B_HEADS)] for p in range(n_pair)]
    kv = [[jnp.concatenate([_dot(kst[p][c][krow[h]], bv_ref[0, rows[p], vcol[h]])
                            for h in range(B_HEADS)], axis=0)
           for c in range(2)] for p in range(n_pair)]

    states = [state_scr[...]]
    for p in range(n_pair):
        for c in range(2):
            states.append(decay[p][c] * states[-1] + kv[p][c])
    state_scr[...] = states[-1]
    sb = [s.astype(BF16) for s in states[:-1]]

    for p in range(n_pair):
        for h in range(B_HEADS):
            am = jnp.where(tril_mask, a[p][h], 0.0).astype(BF16)
            o = _dot(am, bv_ref[0, rows[p], vcol[h]]) + jnp.concatenate(
                [_dot(qh[p][h][:CHUNK], sb[2 * p]), _dot(qh[p][h][CHUNK:], sb[2 * p + 1])], axis=0)
            o_ref[0, rows[p], vcol[h]] = o.astype(BF16)


def _gla(bq, bkt, bv, gcum):
    B, S, _ = bq.shape
    grid = (B, S // TG)
    tokspec = lambda w: pl.BlockSpec((1, TG, w), lambda b, i: (b, i, 0))
    featspec = pl.BlockSpec((1, TG // PAIR, 256, PAIR), lambda b, i: (b, i, 0, 0))
    return pl.pallas_call(
        _gla_body,
        out_shape=jax.ShapeDtypeStruct((B, S, B_WIDTH), BF16),
        grid=grid,
        in_specs=[tokspec(256), featspec, tokspec(512), tokspec(256)],
        out_specs=tokspec(512),
        scratch_shapes=[pltpu.VMEM((B_HEADS * B_K_DIM, B_V_DIM), F32)],
        compiler_params=pltpu.CompilerParams(
            dimension_semantics=("parallel", "arbitrary"),
            vmem_limit_bytes=VMEM_LIMIT_BYTES),
        name="gla",
    )(bq, bkt, bv, gcum)


def _merge_body(x_ref, ya_ref, ob_ref, br_ref, g_ref, og_ref, wb_ref, wo_ref, nf_ref, wu_ref,
                wd_ref, o_ref):
    og = og_ref[...]
    yb = []
    for h in range(B_HEADS):
        cols = slice(h * B_V_DIM, (h + 1) * B_V_DIM)
        o = ob_ref[:, cols].astype(F32)
        gate = br_ref[:, cols].astype(F32)
        ms_o = jnp.mean(o * o, axis=-1, keepdims=True)
        yb.append((o * lax.rsqrt(ms_o + EPS) * og * gate).astype(BF16))
    pa = _dot(ya_ref[...], wb_ref[0])
    pb = _dot(jnp.concatenate(yb, axis=-1), wb_ref[1])
    ga = g_ref[:, :D_MODEL].astype(F32)
    gb = g_ref[:, D_MODEL:].astype(F32)
    mixed = (ga * pa + gb * pb).astype(BF16)
    x1 = x_ref[...] + _dot(mixed, wo_ref[...])
    ms = jnp.mean(x1 * x1, axis=-1, keepdims=True)
    h = (x1 * lax.rsqrt(ms + EPS) * nf_ref[...]).astype(BF16)
    acc = x1
    for c in range(D_FF // FF_CHUNK):
        up = _dot(h, wu_ref[:, c * FF_CHUNK:(c + 1) * FF_CHUNK])
        act = jnp.square(jnp.maximum(up, 0.0)).astype(BF16)
        acc = acc + _dot(act, wd_ref[c * FF_CHUNK:(c + 1) * FF_CHUNK, :])
    o_ref[...] = acc


def _merge(x2, ya2, ob2, br2, gates2, og, wb, wo, nf, wu, wd):
    T, D = x2.shape
    tm = TM_MERGE
    row = lambda w: pl.BlockSpec((tm, w), lambda i: (i, 0))
    const = lambda a: pl.BlockSpec(a.shape, lambda i: (0,) * a.ndim,
                                   pipeline_mode=pl.Buffered(1))
    return pl.pallas_call(
        _merge_body,
        out_shape=jax.ShapeDtypeStruct((T, D), F32),
        grid=(T // tm,),
        in_specs=[row(D), row(A_WIDTH), row(B_WIDTH), row(B_WIDTH), row(2 * D), const(og),
                  const(wb), const(wo), const(nf), const(wu), const(wd)],
        out_specs=row(D),
        compiler_params=pltpu.CompilerParams(
            dimension_semantics=("parallel",),
            vmem_limit_bytes=VMEM_LIMIT_BYTES),
        name="merge_mlp",
    )(x2, ya2, ob2, br2, gates2, og, wb, wo, nf, wu, wd)


def _split_w_in(w_in):
    widths = (512, 512, 512, 256, 256, 512, B_GATE_RANK, 512, 2 * D_MODEL)
    parts, start = [], 0
    for w in widths:
        parts.append(w_in[..., start:start + w])
        start += w
    return parts


def kernel(x, norm_mix, w_in, a_q_norm, a_k_norm, a_lambda_q1, a_lambda_k1, a_lambda_q2,
           a_lambda_k2, a_sub_norm, b_gate_up, b_gate_bias, b_out_norm, w_branch, w_out,
           norm_ffn, w_up, w_down):
    B, S, D = x.shape
    L = DEPTH
    aq, ak, av, bq, bk, bv, ba, br, gates = _split_w_in(w_in.astype(BF16))
    ba_pad = jnp.pad(ba, ((0, 0), (0, 0), (0, LANES - B_GATE_RANK)))
    wtok = jnp.concatenate([ak, bq, bv, br, gates, ba_pad], axis=-1)
    wfeat = jnp.swapaxes(jnp.concatenate([aq, av, bk], axis=-1), 1, 2)
    wgu = jnp.pad(b_gate_up, ((0, 0), (0, LANES - B_GATE_RANK), (0, 0))).astype(BF16)
    bg = b_gate_bias[:, None, :]
    nw = norm_mix[:, None, :]
    nf = norm_ffn[:, None, :]
    q_fold = A_QK_DIM ** -0.5 * LOG2_E
    gq_col = (a_q_norm * q_fold)[:, :, None]
    gk_row = jnp.tile(a_k_norm, (1, 2))[:, None, :]
    score_bound = (A_QK_DIM * q_fold * jnp.max(jnp.abs(a_q_norm), axis=1)
                   * jnp.max(jnp.abs(a_k_norm), axis=1))
    fast = (score_bound <= SAFE_LOG2_SCORE).astype(jnp.int32)[:, None]
    sg_col = a_sub_norm[:, :, None]
    og_row = b_out_norm[:, None, :]
    lam_params = jnp.stack([a_lambda_q1, a_lambda_k1, a_lambda_q2, a_lambda_k2], axis=1)
    wb = w_branch.astype(BF16)
    wo = w_out.astype(BF16)
    wu = w_up.astype(BF16)
    wd = w_down.astype(BF16)

    for l in range(L):
        linit = jnp.full((1,), 0.8 - 0.6 * math.exp(-0.3 * l), F32)
        (khat, bql, bvl, brl, gl, gcum, qbd, avt, bkt) = _in_proj(
            x, nw[l], wtok[l], wfeat[l], wgu[l], bg[l], gq_col[l], gk_row[l])
        ya = _attn(linit, fast[l], lam_params[l], sg_col[l], qbd, khat, avt)
        ob = _gla(bql, bkt, bvl, gcum)
        x = _merge(x.reshape(B * S, D), ya.reshape(B * S, A_WIDTH), ob.reshape(B * S, B_WIDTH),
                   brl.reshape(B * S, B_WIDTH), gl.reshape(B * S, 2 * D), og_row[l],
                   wb[l], wo[l], nf[l], wu[l], wd[l]).reshape(B, S, D)
    return x
```

```python
import functools
import math

import jax
import jax.numpy as jnp
from jax import lax
from jax.experimental import pallas as pl
from jax.experimental.pallas import tpu as pltpu

D_MODEL = 1024
DEPTH = 4
CHUNK = 64
EPS = 1e-6
A_HEADS = 4
A_QK_DIM = 64
A_V_DIM = 128
A_WIDTH = 512
B_HEADS = 4
B_K_DIM = 64
B_V_DIM = 128
B_WIDTH = 512
B_GATE_RANK = 16
B_GATE_TAU = 16.0
D_FF = 4 * D_MODEL

VMEM_LIMIT_BYTES = 56 * 1024 * 1024
LANES = 128

TM_PROJ = 512
TQ = 256
TK = 256
KV_PER_ITER = 4
PAIR = 2 * CHUNK
TG = 512
TM_MERGE = 512
FF_CHUNK = 1024
NEG_BIG = -1e30
LOG2_E = 1.4426950408889634
SAFE_LOG2_SCORE = 100.0

BF16 = jnp.bfloat16
F32 = jnp.float32


def _dot(a, b):
    return jnp.dot(a, b, preferred_element_type=F32)


def _dot_nt(a, b):
    return lax.dot_general(a, b, (((1,), (1,)), ((), ())), preferred_element_type=F32)


def _sigmoid(x):
    return 0.5 * jnp.tanh(0.5 * x) + 0.5


def _log_sigmoid(z):
    return jnp.minimum(z, 0.0) - jnp.log(1.0 + jnp.exp(-jnp.abs(z)))


def _split2(g):
    hi = g.astype(BF16)
    lo = (g - hi.astype(F32)).astype(BF16)
    return hi, lo


def _in_proj_body(x_ref, nw_ref, wtok_ref, wfeat_ref, wgu_ref, bg_ref, gq_ref, gk_ref,
                  khat_ref, bq_ref, bv_ref, br_ref, gates_ref, gcum_ref,
                  qbd_ref, avt_ref, bkt_ref):
    x = x_ref[0]
    ms = jnp.mean(x * x, axis=-1, keepdims=True)
    u = (x * lax.rsqrt(ms + EPS) * nw_ref[...]).astype(BF16)
    n_q = TM_PROJ // TQ

    def tok(start, width):
        return _dot(u, wtok_ref[:, start:start + width])

    ak = tok(0, 512)
    lane = lax.broadcasted_iota(jnp.int32, (TM_PROJ, LANES), 1)
    first_map = lane < A_QK_DIM
    inv_d = 1.0 / A_QK_DIM
    for h in range(A_HEADS):
        kk = ak[:, h * LANES:(h + 1) * LANES]
        k2 = kk * kk
        s1 = jnp.sum(jnp.where(first_map, k2, 0.0), axis=-1, keepdims=True)
        s2 = jnp.sum(jnp.where(first_map, 0.0, k2), axis=-1, keepdims=True)
        r = jnp.where(first_map, lax.rsqrt(s1 * inv_d + EPS), lax.rsqrt(s2 * inv_d + EPS))
        khat = (kk * r * gk_ref[...]).astype(BF16)
        for s in range(n_q):
            khat_ref[0, s, h] = khat[s * TK:(s + 1) * TK]
    bq_ref[0] = tok(512, 256).astype(BF16)
    bv_ref[0] = tok(768, 512).astype(BF16)
    br = tok(1280, 512)
    br_ref[0] = (br * _sigmoid(br)).astype(BF16)
    for c in range(4):
        gates_ref[0, :, c * 512:(c + 1) * 512] = _sigmoid(tok(1792 + c * 512, 512)).astype(BF16)
    ba = tok(3840, LANES).astype(BF16)
    z = _dot(ba, wgu_ref[...]) + bg_ref[...]
    g_hi, g_lo = _split2(_log_sigmoid(z) * (1.0 / B_GATE_TAU))
    r = lax.broadcasted_iota(jnp.int32, (PAIR, PAIR), 0)
    c = lax.broadcasted_iota(jnp.int32, (PAIR, PAIR), 1)
    same_chunk = lax.shift_right_logical(r, 6) == lax.shift_right_logical(c, 6)
    tril = jnp.where(same_chunk & (c <= r), 1.0, 0.0).astype(BF16)
    for s in range(TM_PROJ // PAIR):
        rows = slice(s * PAIR, (s + 1) * PAIR)
        gcum_ref[0, rows, :] = _dot(tril, g_hi[rows]) + _dot(tril, g_lo[rows])

    def feat(start, rows):
        return _dot_nt(wfeat_ref[start:start + rows, :], u)

    aqt = feat(0, 512)
    zeros = jnp.zeros((A_QK_DIM, TQ), BF16)
    for g in range(2 * A_HEADS):
        h, m = divmod(g, 2)
        qg = aqt[g * A_QK_DIM:(g + 1) * A_QK_DIM]
        rq = lax.rsqrt(jnp.mean(qg * qg, axis=0, keepdims=True) + EPS)
        qhat = (qg * rq * gq_ref[...]).astype(BF16)
        for s in range(n_q):
            rows = slice(m * A_QK_DIM, (m + 1) * A_QK_DIM)
            qbd_ref[0, s, h, rows, m * TQ:(m + 1) * TQ] = qhat[:, s * TQ:(s + 1) * TQ]
            qbd_ref[0, s, h, rows, (1 - m) * TQ:(2 - m) * TQ] = zeros
    avt = feat(512, 512).astype(BF16)
    for s in range(n_q):
        avt_ref[0, s] = avt[:, s * TQ:(s + 1) * TQ]
    bkt = feat(1024, 256).astype(BF16)
    for s in range(TM_PROJ // PAIR):
        bkt_ref[0, s] = bkt[:, s * PAIR:(s + 1) * PAIR]


def _in_proj(x, nw, wtok, wfeat, wgu, bg, gq_col, gk_row):
    B, S, D = x.shape
    tm = TM_PROJ
    grid = (B, S // tm)
    const2 = lambda b, i: (0, 0)
    tokspec = lambda w: pl.BlockSpec((1, tm, w), lambda b, i: (b, i, 0))
    out_shape = (
        jax.ShapeDtypeStruct((B, S // TK, A_HEADS, TK, LANES), BF16),
        jax.ShapeDtypeStruct((B, S, 256), BF16),
        jax.ShapeDtypeStruct((B, S, 512), BF16),
        jax.ShapeDtypeStruct((B, S, 512), BF16),
        jax.ShapeDtypeStruct((B, S, 2 * D_MODEL), BF16),
        jax.ShapeDtypeStruct((B, S, 256), F32),
        jax.ShapeDtypeStruct((B, S // TQ, A_HEADS, LANES, 2 * TQ), BF16),
        jax.ShapeDtypeStruct((B, S // TQ, 512, TQ), BF16),
        jax.ShapeDtypeStruct((B, S // PAIR, 256, PAIR), BF16),
    )
    out_specs = (
        pl.BlockSpec((1, tm // TK, A_HEADS, TK, LANES), lambda b, i: (b, i, 0, 0, 0)),
        tokspec(256), tokspec(512), tokspec(512), tokspec(2 * D_MODEL), tokspec(256),
        pl.BlockSpec((1, tm // TQ, A_HEADS, LANES, 2 * TQ), lambda b, i: (b, i, 0, 0, 0)),
        pl.BlockSpec((1, tm // TQ, 512, TQ), lambda b, i: (b, i, 0, 0)),
        pl.BlockSpec((1, tm // PAIR, 256, PAIR), lambda b, i: (b, i, 0, 0)),
    )
    in_specs = [
        pl.BlockSpec((1, tm, D), lambda b, i: (b, i, 0)),
        pl.BlockSpec(nw.shape, const2),
        pl.BlockSpec(wtok.shape, const2),
        pl.BlockSpec(wfeat.shape, const2),
        pl.BlockSpec(wgu.shape, const2),
        pl.BlockSpec(bg.shape, const2),
        pl.BlockSpec(gq_col.shape, const2),
        pl.BlockSpec(gk_row.shape, const2),
    ]
    return pl.pallas_call(
        _in_proj_body,
        out_shape=out_shape,
        grid=grid,
        in_specs=in_specs,
        out_specs=out_specs,
        compiler_params=pltpu.CompilerParams(
            dimension_semantics=("parallel", "arbitrary"),
            vmem_limit_bytes=VMEM_LIMIT_BYTES),
        name="in_proj",
    )(x, nw, wtok, wfeat, wgu, bg, gq_col, gk_row)


def _attn_body(linit_ref, fast_ref, lam_ref, sg_ref, qbd_ref, khat_ref, vt_ref, o_ref,
               m_scr, l_scr, acc_scr):
    i = pl.program_id(1)
    shape = (TK, 2 * TQ)
    kc = lax.shift_right_logical(lax.broadcasted_iota(jnp.int32, shape, 0), 6)
    qc = lax.shift_right_logical(lax.broadcasted_iota(jnp.int32, shape, 1) & (TQ - 1), 6)
    allowed = kc <= qc

    l_scr[...] = jnp.zeros(l_scr.shape, F32)
    acc_scr[...] = jnp.zeros(acc_scr.shape, F32)

    def scores(j, h):
        return _dot(khat_ref[0, j, h], qbd_ref[0, 0, h])

    def values(j, h):
        return vt_ref[0, j, h * LANES:(h + 1) * LANES, :]

    def fast_blocks(blocks):
        ss = [[scores(j, h) for h in range(A_HEADS)] for j, _ in blocks]
        for (j, masked), sj in zip(blocks, ss):
            for h in range(A_HEADS):
                s = jnp.where(allowed, sj[h], NEG_BIG) if masked else sj[h]
                p = jnp.exp2(s)
                l_scr[h] += jnp.sum(p, axis=0, keepdims=True)
                acc_scr[h] += _dot(values(j, h), p.astype(BF16))

    @pl.when(fast_ref[0] != 0)
    def _():
        n_groups = i // KV_PER_ITER

        def group(jj, carry):
            fast_blocks([(KV_PER_ITER * jj + k, False) for k in range(KV_PER_ITER)])
            return carry
        lax.fori_loop(0, n_groups, group, 0)

        left = i - KV_PER_ITER * n_groups
        for n_left in range(KV_PER_ITER):
            @pl.when(left == n_left)
            def _(n_left=n_left):
                fast_blocks([(i - n_left + k, False) for k in range(n_left)] + [(i, True)])

    @pl.when(fast_ref[0] == 0)
    def _():
        m_scr[...] = jnp.full(m_scr.shape, NEG_BIG, F32)

        def online_block(j, carry):
            keep = jnp.logical_or(allowed, j < i)
            for h in range(A_HEADS):
                s = jnp.where(keep, scores(j, h), NEG_BIG)
                m_old = m_scr[h]
                m_new = jnp.maximum(m_old, jnp.max(s, axis=0, keepdims=True))
                alpha = jnp.exp2(m_old - m_new)
                p = jnp.exp2(s - m_new)
                l_scr[h] = alpha * l_scr[h] + jnp.sum(p, axis=0, keepdims=True)
                m_scr[h] = m_new
                acc_scr[h] = alpha * acc_scr[h] + _dot(values(j, h), p.astype(BF16))
            return carry
        lax.fori_loop(0, i + 1, online_block, 0)

    lp = lam_ref[...]
    d1 = jnp.sum(lp[0:1] * lp[1:2], axis=-1, keepdims=True)
    d2 = jnp.sum(lp[2:3] * lp[3:4], axis=-1, keepdims=True)
    linit = linit_ref[0]
    lam = jnp.exp(d1) - jnp.exp(d2) + linit
    for h in range(A_HEADS):
        inv_l = 1.0 / l_scr[h]
        acc = acc_scr[h]
        o = acc[:, :TQ] * inv_l[:, :TQ] - lam * (acc[:, TQ:] * inv_l[:, TQ:])
        ms = jnp.mean(o * o, axis=0, keepdims=True)
        y = o * lax.rsqrt(ms + EPS) * sg_ref[...] * (1.0 - linit)
        o_ref[0, :, h * LANES:(h + 1) * LANES] = y.T.astype(BF16)


def _attn(linit, fast, lam_params, sg_col, qbd, khat, avt):
    B, n_kv = khat.shape[:2]
    S = n_kv * TK
    n_q = S // TQ
    grid = (B, n_q)
    const2 = lambda b, i: (0, 0)
    in_specs = [
        pl.BlockSpec(memory_space=pltpu.SMEM),
        pl.BlockSpec(memory_space=pltpu.SMEM),
        pl.BlockSpec(lam_params.shape, const2),
        pl.BlockSpec(sg_col.shape, const2),
        pl.BlockSpec((1, 1, A_HEADS, LANES, 2 * TQ), lambda b, i: (b, i, 0, 0, 0)),
        pl.BlockSpec((1, n_kv, A_HEADS, TK, LANES), lambda b, i: (b, 0, 0, 0, 0)),
        pl.BlockSpec((1, n_kv, A_WIDTH, TK), lambda b, i: (b, 0, 0, 0)),
    ]
    return pl.pallas_call(
        _attn_body,
        out_shape=jax.ShapeDtypeStruct((B, S, A_WIDTH), BF16),
        grid=grid,
        in_specs=in_specs,
        out_specs=pl.BlockSpec((1, TQ, A_WIDTH), lambda b, i: (b, i, 0)),
        scratch_shapes=[
            pltpu.VMEM((A_HEADS, 1, 2 * TQ), F32),
            pltpu.VMEM((A_HEADS, 1, 2 * TQ), F32),
            pltpu.VMEM((A_HEADS, LANES, 2 * TQ), F32),
        ],
        compiler_params=pltpu.CompilerParams(
            dimension_semantics=("parallel", "arbitrary"),
            vmem_limit_bytes=VMEM_LIMIT_BYTES),
        name="diff_attn",
    )(linit, fast, lam_params, sg_col, qbd, khat, avt)


def _gla_body(bq_ref, bkt_ref, bv_ref, gcum_ref, o_ref, state_scr):
    @pl.when(pl.program_id(1) == 0)
    def _():
        state_scr[...] = jnp.zeros(state_scr.shape, F32)

    n_pair = TG // PAIR
    hdk = B_HEADS * B_K_DIM
    r = lax.broadcasted_iota(jnp.int32, (PAIR, PAIR), 0)
    c = lax.broadcasted_iota(jnp.int32, (PAIR, PAIR), 1)
    same = lax.shift_right_logical(r, 6) == lax.shift_right_logical(c, 6)
    tril_mask = same & (c <= r)
    lane_q = lax.broadcasted_iota(jnp.int32, (PAIR, hdk), 1)
    first_chunk = lax.broadcasted_iota(jnp.int32, (hdk, PAIR), 1) < CHUNK
    heads = [(lane_q >= h * B_K_DIM) & (lane_q < (h + 1) * B_K_DIM) for h in range(B_HEADS)]
    rows = [slice(p * PAIR, (p + 1) * PAIR) for p in range(n_pair)]
    vcol = [slice(h * B_V_DIM, (h + 1) * B_V_DIM) for h in range(B_HEADS)]
    krow = [slice(h * B_K_DIM, (h + 1) * B_K_DIM) for h in range(B_HEADS)]

    q_dec, kdt, kst, decay = [], [], [], []
    for p in range(n_pair):
        Gp = gcum_ref[0, rows[p], :]
        q_dec.append((bq_ref[0, rows[p], :].astype(F32) * (B_K_DIM ** -0.5)
                      * jnp.exp(Gp)).astype(BF16))
        Gt = Gp.T
        gl0 = jnp.broadcast_to(Gp[CHUNK - 1:CHUNK, :], (PAIR, hdk)).T
        gl1 = jnp.broadcast_to(Gp[PAIR - 1:PAIR, :], (PAIR, hdk)).T
        kd = bkt_ref[0, p].astype(F32) * jnp.exp(-Gt)
        ks = kd * jnp.exp(jnp.where(first_chunk, gl0, gl1))
        kdt.append(kd.astype(BF16))
        kst.append((jnp.where(first_chunk, ks, 0.0).astype(BF16),
                    jnp.where(first_chunk, 0.0, ks).astype(BF16)))
        decay.append((jnp.exp(gl0), jnp.exp(gl1)))

    qh = [[jnp.where(heads[h], q_dec[p], jnp.zeros_like(q_dec[p])) for h in range(B_HEADS)]
          for p in range(n_pair)]
    a = [[_dot(qh[p][h], kdt[p]) for h in range(B_HEADS)] for p in range(n_pair)]
    kv = [[jnp.concatenate([_dot(kst[p][c][krow[h]], bv_ref[0, rows[p], vcol[h]])
                            for h in range(B_HEADS)], axis=0)
           for c in range(2)] for p in range(n_pair)]

    states = [state_scr[...]]
    for p in range(n_pair):
        for c in range(2):
            states.append(decay[p][c] * states[-1] + kv[p][c])
    state_scr[...] = states[-1]
    sb = [s.astype(BF16) for s in states[:-1]]

    for p in range(n_pair):
        for h in range(B_HEADS):
            am = jnp.where(tril_mask, a[p][h], 0.0).astype(BF16)
            o = _dot(am, bv_ref[0, rows[p], vcol[h]]) + jnp.concatenate(
                [_dot(qh[p][h][:CHUNK], sb[2 * p]), _dot(qh[p][h][CHUNK:], sb[2 * p + 1])], axis=0)
            o_ref[0, rows[p], vcol[h]] = o.astype(BF16)


def _gla(bq, bkt, bv, gcum):
    B, S, _ = bq.shape
    grid = (B, S // TG)
    tokspec = lambda w: pl.BlockSpec((1, TG, w), lambda b, i: (b, i, 0))
    featspec = pl.BlockSpec((1, TG // PAIR, 256, PAIR), lambda b, i: (b, i, 0, 0))
    return pl.pallas_call(
        _gla_body,
        out_shape=jax.ShapeDtypeStruct((B, S, B_WIDTH), BF16),
        grid=grid,
        in_specs=[tokspec(256), featspec, tokspec(512), tokspec(256)],
        out_specs=tokspec(512),
        scratch_shapes=[pltpu.VMEM((B_HEADS * B_K_DIM, B_V_DIM), F32)],
        compiler_params=pltpu.CompilerParams(
            dimension_semantics=("parallel", "arbitrary"),
            vmem_limit_bytes=VMEM_LIMIT_BYTES),
        name="gla",
    )(bq, bkt, bv, gcum)


def _merge_body(x_ref, ya_ref, ob_ref, br_ref, g_ref, og_ref, wb_ref, wo_ref, nf_ref, wu_ref,
                wd_ref, o_ref):
    og = og_ref[...]
    yb = []
    for h in range(B_HEADS):
        cols = slice(h * B_V_DIM, (h + 1) * B_V_DIM)
        o = ob_ref[:, cols].astype(F32)
        gate = br_ref[:, cols].astype(F32)
        ms_o = jnp.mean(o * o, axis=-1, keepdims=True)
        yb.append((o * lax.rsqrt(ms_o + EPS) * og * gate).astype(BF16))
    pa = _dot(ya_ref[...], wb_ref[0])
    pb = _dot(jnp.concatenate(yb, axis=-1), wb_ref[1])
    ga = g_ref[:, :D_MODEL].astype(F32)
    gb = g_ref[:, D_MODEL:].astype(F32)
    mixed = (ga * pa + gb * pb).astype(BF16)
    x1 = x_ref[...] + _dot(mixed, wo_ref[...])
    ms = jnp.mean(x1 * x1, axis=-1, keepdims=True)
    h = (x1 * lax.rsqrt(ms + EPS) * nf_ref[...]).astype(BF16)
    acc = x1
    for c in range(D_FF // FF_CHUNK):
        up = _dot(h, wu_ref[:, c * FF_CHUNK:(c + 1) * FF_CHUNK])
        act = jnp.square(jnp.maximum(up, 0.0)).astype(BF16)
        acc = acc + _dot(act, wd_ref[c * FF_CHUNK:(c + 1) * FF_CHUNK, :])
    o_ref[...] = acc


def _merge(x2, ya2, ob2, br2, gates2, og, wb, wo, nf, wu, wd):
    T, D = x2.shape
    tm = TM_MERGE
    row = lambda w: pl.BlockSpec((tm, w), lambda i: (i, 0))
    const = lambda a: pl.BlockSpec(a.shape, lambda i: (0,) * a.ndim,
                                   pipeline_mode=pl.Buffered(1))
    return pl.pallas_call(
        _merge_body,
        out_shape=jax.ShapeDtypeStruct((T, D), F32),
        grid=(T // tm,),
        in_specs=[row(D), row(A_WIDTH), row(B_WIDTH), row(B_WIDTH), row(2 * D), const(og),
                  const(wb), const(wo), const(nf), const(wu), const(wd)],
        out_specs=row(D),
        compiler_params=pltpu.CompilerParams(
            dimension_semantics=("parallel",),
            vmem_limit_bytes=VMEM_LIMIT_BYTES),
        name="merge_mlp",
    )(x2, ya2, ob2, br2, gates2, og, wb, wo, nf, wu, wd)


def _split_w_in(w_in):
    widths = (512, 512, 512, 256, 256, 512, B_GATE_RANK, 512, 2 * D_MODEL)
    parts, start = [], 0
    for w in widths:
        parts.append(w_in[..., start:start + w])
        start += w
    return parts


def kernel(x, norm_mix, w_in, a_q_norm, a_k_norm, a_lambda_q1, a_lambda_k1, a_lambda_q2,
           a_lambda_k2, a_sub_norm, b_gate_up, b_gate_bias, b_out_norm, w_branch, w_out,
           norm_ffn, w_up, w_down):
    B, S, D = x.shape
    L = DEPTH
    aq, ak, av, bq, bk, bv, ba, br, gates = _split_w_in(w_in.astype(BF16))
    ba_pad = jnp.pad(ba, ((0, 0), (0, 0), (0, LANES - B_GATE_RANK)))
    wtok = jnp.concatenate([ak, bq, bv, br, gates, ba_pad], axis=-1)
    wfeat = jnp.swapaxes(jnp.concatenate([aq, av, bk], axis=-1), 1, 2)
    wgu = jnp.pad(b_gate_up, ((0, 0), (0, LANES - B_GATE_RANK), (0, 0))).astype(BF16)
    bg = b_gate_bias[:, None, :]
    nw = norm_mix[:, None, :]
    nf = norm_ffn[:, None, :]
    q_fold = A_QK_DIM ** -0.5 * LOG2_E
    gq_col = (a_q_norm * q_fold)[:, :, None]
    gk_row = jnp.tile(a_k_norm, (1, 2))[:, None, :]
    score_bound = (A_QK_DIM * q_fold * jnp.max(jnp.abs(a_q_norm), axis=1)
                   * jnp.max(jnp.abs(a_k_norm), axis=1))
    fast = (score_bound <= SAFE_LOG2_SCORE).astype(jnp.int32)[:, None]
    sg_col = a_sub_norm[:, :, None]
    og_row = b_out_norm[:, None, :]
    lam_params = jnp.stack([a_lambda_q1, a_lambda_k1, a_lambda_q2, a_lambda_k2], axis=1)
    wb = w_branch.astype(BF16)
    wo = w_out.astype(BF16)
    wu = w_up.astype(BF16)
    wd = w_down.astype(BF16)

    for l in range(L):
        linit = jnp.full((1,), 0.8 - 0.6 * math.exp(-0.3 * l), F32)
        (khat, bql, bvl, brl, gl, gcum, qbd, avt, bkt) = _in_proj(
            x, nw[l], wtok[l], wfeat[l], wgu[l], bg[l], gq_col[l], gk_row[l])
        ya = _attn(linit, fast[l], lam_params[l], sg_col[l], qbd, khat, avt)
        ob = _gla(bql, bkt, bvl, gcum)
        x = _merge(x.reshape(B * S, D), ya.reshape(B * S, A_WIDTH), ob.reshape(B * S, B_WIDTH),
                   brl.reshape(B * S, B_WIDTH), gl.reshape(B * S, 2 * D), og_row[l],
                   wb[l], wo[l], nf[l], wu[l], wd[l]).reshape(B, S, D)
    return x
```
